```python
import math
import jax, jax.numpy as jnp
from jax import lax
import numpy as np

D_MODEL = 2048
BATCH = 4
SEQ = 2048
DEPTH = 4

GRID_W = 64
CTX_LEN = 256
HEAD_DIM = 128
N_GROUPS = 4
GROUP_HEADS = D_MODEL // (N_GROUPS * HEAD_DIM)
GROUP_WIDTH = GROUP_HEADS * HEAD_DIM
MIX_WIDTH = N_GROUPS * GROUP_WIDTH
KV_HEADS = 2
KV_WIDTH = KV_HEADS * HEAD_DIM
DIFF_DIM = HEAD_DIM // 2
WINDOW = 128
Q_BLOCK = 128
NA_WIN_H = 8
NA_WIN_W = 16
NA_COL_BLOCK = 16
NA_KEY_COLS = NA_COL_BLOCK + NA_WIN_W
ROPE_THETA = 10000.0
EPS = 1e-6
NEG_INF = -1e30

IN_SIZES = (GROUP_WIDTH, GROUP_WIDTH, GROUP_WIDTH,
            GROUP_WIDTH, KV_WIDTH, KV_WIDTH,
            GROUP_WIDTH, GROUP_WIDTH, GROUP_WIDTH,
            GROUP_WIDTH, KV_WIDTH, KV_WIDTH,
            MIX_WIDTH)
IN_WIDTH = sum(IN_SIZES)

kernel_name = 'hybrid_parallel_group_dit_block'


def rms_norm(x, g):
    xf = x.astype(jnp.float32)
    y = xf * lax.rsqrt(jnp.mean(xf * xf, axis=-1, keepdims=True) + EPS)
    return (y * g.astype(jnp.float32)).astype(x.dtype)


def axial_rope_tables(n_tokens, dim, dtype):
    t = jnp.arange(n_tokens)
    row = (t // GRID_W).astype(jnp.float32)
    col = (t % GRID_W).astype(jnp.float32)
    n_freq = dim // 4
    inv = ROPE_THETA ** (-jnp.arange(n_freq, dtype=jnp.float32) / n_freq)
    ang = jnp.concatenate([row[:, None] * inv, col[:, None] * inv], axis=-1)
    return jnp.cos(ang).astype(dtype), jnp.sin(ang).astype(dtype)


def apply_rope(x, cos, sin):
    h = x.shape[-1] // 2
    x1, x2 = x[..., :h], x[..., h:]
    return jnp.concatenate([x1 * cos - x2 * sin, x2 * cos + x1 * sin], axis=-1)


def to_heads(t, n_heads, dim):
    b, s, _ = t.shape
    return t.reshape(b, s, n_heads, dim).transpose(0, 2, 1, 3)


def from_heads(t):
    b, h, s, d = t.shape
    return t.transpose(0, 2, 1, 3).reshape(b, s, h * d)


def diff_attention(q_lat, k_lat, v_lat, q_ctx, k_ctx, v_ctx, lam, subln_g, lambda_init, rope, with_ctx):
    cos, sin = rope
    b, s, _ = q_lat.shape
    H = GROUP_HEADS
    nblk = s // Q_BLOCK
    scale = DIFF_DIM ** -0.5

    def split2(t):
        n = t.shape[1]
        return t.reshape(b, n, H, 2, DIFF_DIM).transpose(0, 2, 3, 1, 4)

    ql = apply_rope(split2(q_lat), cos, sin)
    kl = apply_rope(split2(k_lat), cos, sin)
    kc = split2(k_ctx)
    vl, vc = to_heads(v_lat, H, HEAD_DIM), to_heads(v_ctx, H, HEAD_DIM)
    k_all = jnp.concatenate([kc, kl], axis=3)
    v_all = jnp.concatenate([vc, vl], axis=2)

    def attend(q, k, v):
        p = jax.nn.softmax(jnp.einsum('bhpqd,bhpkd->bhpqk', q, k).astype(jnp.float32) * scale, axis=-1)
        a = p[:, :, 0] - lam * p[:, :, 1]
        return jnp.einsum('bhqk,bhkd->bhqd', a.astype(v.dtype), v)

    qb = jnp.moveaxis(ql.reshape(b, H, 2, nblk, Q_BLOCK, DIFF_DIM), 3, 0)
    ob = lax.map(lambda qq: attend(qq, k_all, v_all), qb)
    o_lat = jnp.moveaxis(ob, 0, 2).reshape(b, H, s, HEAD_DIM)

    def post(o):
        return from_heads(rms_norm(o, subln_g) * (1.0 - lambda_init))

    if not with_ctx:
        return post(o_lat), None
    return post(o_lat), post(attend(split2(q_ctx), kc, vc))


def _band(t, nblk, side):
    b, hk, s, d = t.shape
    tp = jnp.pad(t, ((0, 0), (0, 0), (side * Q_BLOCK, side * Q_BLOCK), (0, 0)))
    tp = tp.reshape(b, hk, nblk + 2 * side, Q_BLOCK, d)
    return jnp.concatenate([tp[:, :, j:j + nblk] for j in range(2 * side + 1)], axis=3)


def window_attention(q_lat, k_lat, v_lat, q_ctx, k_ctx, v_ctx, sink, rope, with_ctx):
    cos, sin = rope
    b, s, _ = q_lat.shape
    L = k_ctx.shape[1]
    g = GROUP_HEADS // KV_HEADS
    nblk = s // Q_BLOCK
    side = WINDOW // Q_BLOCK
    nk = (2 * side + 1) * Q_BLOCK
    scale = HEAD_DIM ** -0.5
    ql = apply_rope(to_heads(q_lat, GROUP_HEADS, HEAD_DIM), cos, sin).reshape(b, KV_HEADS, g, nblk, Q_BLOCK, HEAD_DIM)
    kl = apply_rope(to_heads(k_lat, KV_HEADS, HEAD_DIM), cos, sin)
    vl = to_heads(v_lat, KV_HEADS, HEAD_DIM)
    kc, vc = to_heads(k_ctx, KV_HEADS, HEAD_DIM), to_heads(v_ctx, KV_HEADS, HEAD_DIM)
    kb, vb = _band(kl, nblk, side), _band(vl, nblk, side)
    n = np.arange(nblk)[:, None, None]
    t = np.arange(Q_BLOCK)[None, :, None]
    j = np.arange(nk)[None, None, :]
    qpos = n * Q_BLOCK + t
    kpos = (n - side) * Q_BLOCK + j
    valid = (np.abs(kpos - qpos) <= WINDOW) & (kpos >= 0) & (kpos < s)
    sink_hg = sink.astype(jnp.float32).reshape(KV_HEADS, g)
    s_band = jnp.where(valid, jnp.einsum('bkgnqd,bknjd->bkgnqj', ql, kb).astype(jnp.float32) * scale, NEG_INF)
    s_ctx = jnp.einsum('bkgnqd,bkld->bkgnql', ql, kc).astype(jnp.float32) * scale
    s_sink = jnp.broadcast_to(sink_hg[None, :, :, None, None, None], s_ctx.shape[:-1] + (1,))
    p = jax.nn.softmax(jnp.concatenate([s_band, s_ctx, s_sink], axis=-1), axis=-1).astype(vl.dtype)
    o = (jnp.einsum('bkgnqj,bknjd->bkgnqd', p[..., :nk], vb)
         + jnp.einsum('bkgnql,bkld->bkgnqd', p[..., nk:nk + L], vc))
    o_lat = from_heads(o.reshape(b, GROUP_HEADS, s, HEAD_DIM))
    if not with_ctx:
        return o_lat, None
    qc = to_heads(q_ctx, GROUP_HEADS, HEAD_DIM).reshape(b, KV_HEADS, g, L, HEAD_DIM)
    sc = jnp.einsum('bkgqd,bkld->bkgql', qc, kc).astype(jnp.float32) * scale
    sink_c = jnp.broadcast_to(sink_hg[None, :, :, None, None], sc.shape[:-1] + (1,))
    pc = jax.nn.softmax(jnp.concatenate([sc, sink_c], axis=-1), axis=-1)[..., :L].astype(vc.dtype)
    oc = jnp.einsum('bkgql,bkld->bkgqd', pc, vc).reshape(b, GROUP_HEADS, L, HEAD_DIM)
    return o_lat, from_heads(oc)


def neighborhood_attention(q_lat, k_lat, v_lat, q_ctx, k_ctx, v_ctx, rpb, with_ctx):
    b, s, _ = q_lat.shape
    rows = s // GRID_W
    H = GROUP_HEADS
    scale = HEAD_DIM ** -0.5
    wr = min(NA_WIN_H, rows)
    n_cb = GRID_W // NA_COL_BLOCK
    K = wr * NA_KEY_COLS
    r = np.arange(rows)
    key_rows = np.clip(r - wr // 2, 0, rows - wr)[:, None] + np.arange(wr)
    cb = np.arange(n_cb)
    key_cols = (np.clip(cb * NA_COL_BLOCK - NA_WIN_W // 2, 0, GRID_W - NA_KEY_COLS)[:, None]
                + np.arange(NA_KEY_COLS))
    qcol = cb[:, None] * NA_COL_BLOCK + np.arange(NA_COL_BLOCK)
    cstart = np.clip(qcol - NA_WIN_W // 2, 0, GRID_W - NA_WIN_W)
    key_idx = (key_rows[:, None, :, None] * GRID_W + key_cols[None, :, None, :]).reshape(rows, n_cb, K)
    col_ok = (key_cols[:, None, :] >= cstart[..., None]) & (key_cols[:, None, :] < cstart[..., None] + NA_WIN_W)
    valid = np.broadcast_to(col_ok[:, :, None, :], (n_cb, NA_COL_BLOCK, wr, NA_KEY_COLS)).reshape(n_cb, NA_COL_BLOCK, K)
    rel_r = key_rows - r[:, None] + NA_WIN_H - 1
    rel_c = np.clip(key_cols[:, None, :] - qcol[..., None] + NA_WIN_W - 1, 0, 2 * NA_WIN_W - 2)
    bias = rpb.astype(jnp.float32)[:, rel_r[:, None, None, :, None], rel_c[None, :, :, None, :]]
    bias = jnp.where(valid, bias.reshape(H, rows, n_cb, NA_COL_BLOCK, K), NEG_INF)

    ql = to_heads(q_lat, H, HEAD_DIM).reshape(b, H, rows, n_cb, NA_COL_BLOCK, HEAD_DIM)
    kl, vl = to_heads(k_lat, H, HEAD_DIM), to_heads(v_lat, H, HEAD_DIM)
    kc, vc = to_heads(k_ctx, H, HEAD_DIM), to_heads(v_ctx, H, HEAD_DIM)
    kg, vg = kl[:, :, key_idx], vl[:, :, key_idx]
    s_nb = jnp.einsum('bhrcqd,bhrckd->bhrcqk', ql, kg).astype(jnp.float32) * scale + bias
    s_ctx = jnp.einsum('bhrcqd,bhld->bhrcql', ql, kc).astype(jnp.float32) * scale
    p = jax.nn.softmax(jnp.concatenate([s_nb, s_ctx], axis=-1), axis=-1).astype(vl.dtype)
    o = (jnp.einsum('bhrcqk,bhrckd->bhrcqd', p[..., :K], vg)
         + jnp.einsum('bhrcql,bhld->bhrcqd', p[..., K:], vc))
    o_lat = from_heads(o.reshape(b, H, s, HEAD_DIM))
    if not with_ctx:
        return o_lat, None
    qc = to_heads(q_ctx, H, HEAD_DIM)
    pc = jax.nn.softmax(jnp.einsum('bhqd,bhld->bhql', qc, kc).astype(jnp.float32) * scale, axis=-1).astype(vc.dtype)
    return o_lat, from_heads(jnp.einsum('bhql,bhld->bhqd', pc, vc))


def qknorm_attention(q_lat, k_lat, v_lat, q_ctx, k_ctx, v_ctx, gq, gk, rope, with_ctx):
    cos, sin = rope
    b, s, _ = q_lat.shape
    L = k_ctx.shape[1]
    g = GROUP_HEADS // KV_HEADS
    nblk = s // Q_BLOCK
    scale = HEAD_DIM ** -0.5
    ql = apply_rope(rms_norm(to_heads(q_lat, GROUP_HEADS, HEAD_DIM), gq), cos, sin).reshape(b, KV_HEADS, g, s, HEAD_DIM)
    kl = apply_rope(rms_norm(to_heads(k_lat, KV_HEADS, HEAD_DIM), gk), cos, sin)
    vl = to_heads(v_lat, KV_HEADS, HEAD_DIM)
    kc = rms_norm(to_heads(k_ctx, KV_HEADS, HEAD_DIM), gk)
    vc = to_heads(v_ctx, KV_HEADS, HEAD_DIM)
    k_all = jnp.concatenate([kc, kl], axis=2)
    v_all = jnp.concatenate([vc, vl], axis=2)

    def attend(q, k, v):
        p = jax.nn.softmax(jnp.einsum('bkgqd,bksd->bkgqs', q, k).astype(jnp.float32) * scale, axis=-1)
        return jnp.einsum('bkgqs,bksd->bkgqd', p.astype(v.dtype), v)

    qb = jnp.moveaxis(ql.reshape(b, KV_HEADS, g, nblk, Q_BLOCK, HEAD_DIM), 3, 0)
    ob = lax.map(lambda qq: attend(qq, k_all, v_all), qb)
    o_lat = from_heads(jnp.moveaxis(ob, 0, 3).reshape(b, GROUP_HEADS, s, HEAD_DIM))
    if not with_ctx:
        return o_lat, None
    qc = rms_norm(to_heads(q_ctx, GROUP_HEADS, HEAD_DIM), gq).reshape(b, KV_HEADS, g, L, HEAD_DIM)
    return o_lat, from_heads(attend(qc, kc, vc).reshape(b, GROUP_HEADS, L, HEAD_DIM))


def setup_inputs(seed: int = 0) -> dict:
    key = jax.random.key(seed)
    ks = jax.random.split(key, 20)
    f32 = jnp.float32
    nrm = lambda k, shape: jax.random.normal(k, shape, f32)
    return {
        'x': nrm(ks[0], (BATCH, SEQ, D_MODEL)),
        'c': nrm(ks[1], (BATCH, D_MODEL)),
        'ctx': nrm(ks[2], (BATCH, CTX_LEN, D_MODEL)),
        'c_ctx': nrm(ks[3], (D_MODEL,)),
        'w_mod': nrm(ks[4], (DEPTH, D_MODEL, 3 * D_MODEL)) * (0.5 * D_MODEL ** -0.5),
        'b_mod': nrm(ks[5], (DEPTH, 3 * D_MODEL)) * 0.02,
        'pre_norm_g': 1.0 + 0.02 * nrm(ks[6], (DEPTH, D_MODEL)),
        'w_in': nrm(ks[7], (DEPTH, D_MODEL, IN_WIDTH)) * D_MODEL ** -0.5,
        'diff_lambda_q1': nrm(ks[8], (DEPTH, DIFF_DIM)) * 0.1,
        'diff_lambda_k1': nrm(ks[9], (DEPTH, DIFF_DIM)) * 0.1,
        'diff_lambda_q2': nrm(ks[10], (DEPTH, DIFF_DIM)) * 0.1,
        'diff_lambda_k2': nrm(ks[11], (DEPTH, DIFF_DIM)) * 0.1,
        'diff_subln_g': 1.0 + 0.02 * nrm(ks[12], (DEPTH, HEAD_DIM)),
        'win_sink': nrm(ks[13], (DEPTH, GROUP_HEADS)) * 0.5,
        'na_rpb': nrm(ks[14], (DEPTH, GROUP_HEADS, 2 * NA_WIN_H - 1, 2 * NA_WIN_W - 1)) * 0.1,
        'qk_q_norm_g': 1.0 + 0.02 * nrm(ks[15], (DEPTH, HEAD_DIM)),
        'qk_k_norm_g': 1.0 + 0.02 * nrm(ks[16], (DEPTH, HEAD_DIM)),
        'w_out': nrm(ks[17], (DEPTH, MIX_WIDTH, D_MODEL)) * MIX_WIDTH ** -0.5,
        'post_norm_g': 1.0 + 0.02 * nrm(ks[18], (DEPTH, D_MODEL)),
    }


def reference(x, c, ctx, c_ctx, w_mod, b_mod, pre_norm_g, w_in, diff_lambda_q1, diff_lambda_k1,
              diff_lambda_q2, diff_lambda_k2, diff_subln_g, win_sink, na_rpb, qk_q_norm_g, qk_k_norm_g,
              w_out, post_norm_g):
    b, s, _ = x.shape
    rope_sub = axial_rope_tables(s, DIFF_DIM, x.dtype)
    rope_head = axial_rope_tables(s, HEAD_DIM, x.dtype)
    split_points = [int(v) for v in np.cumsum(IN_SIZES)[:-1]]
    silu_c = jax.nn.silu(c)
    silu_cc = jax.nn.silu(c_ctx)
    h, hc = x, ctx
    for l in range(DEPTH):
        with_ctx = l < DEPTH - 1
        shift, scale, gate = jnp.split(silu_c @ w_mod[l] + b_mod[l], 3, axis=-1)
        shift_c, scale_c, gate_c = jnp.split(silu_cc @ w_mod[l] + b_mod[l], 3, axis=-1)
        n = rms_norm(h, pre_norm_g[l]) * (1.0 + scale[:, None]) + shift[:, None]
        nc = rms_norm(hc, pre_norm_g[l]) * (1.0 + scale_c) + shift_c
        p = jnp.split(n @ w_in[l], split_points, axis=-1)
        pc = jnp.split(nc @ w_in[l], split_points, axis=-1)

        lambda_init = 0.8 - 0.6 * math.exp(-0.3 * l)
        lam = (jnp.exp(jnp.sum(diff_lambda_q1[l].astype(jnp.float32) * diff_lambda_k1[l].astype(jnp.float32)))
               - jnp.exp(jnp.sum(diff_lambda_q2[l].astype(jnp.float32) * diff_lambda_k2[l].astype(jnp.float32)))
               + lambda_init)
        oa, oa_c = diff_attention(p[0], p[1], p[2], pc[0], pc[1], pc[2], lam, diff_subln_g[l], lambda_init,
                                  rope_sub, with_ctx)
        ob, ob_c = window_attention(p[3], p[4], p[5], pc[3], pc[4], pc[5], win_sink[l], rope_head, with_ctx)
        on, on_c = neighborhood_attention(p[6], p[7], p[8], pc[6], pc[7], pc[8], na_rpb[l], with_ctx)
        od, od_c = qknorm_attention(p[9], p[10], p[11], pc[9], pc[10], pc[11], qk_q_norm_g[l], qk_k_norm_g[l],
                                    rope_head, with_ctx)

        y = (jnp.concatenate([oa, ob, on, od], axis=-1) * jax.nn.silu(p[12])) @ w_out[l]
        h = h + gate[:, None] * rms_norm(y, post_norm_g[l])
        if with_ctx:
            yc = (jnp.concatenate([oa_c, ob_c, on_c, od_c], axis=-1) * jax.nn.silu(pc[12])) @ w_out[l]
            hc = hc + gate_c * rms_norm(yc, post_norm_g[l])
    return h
```

```python
import functools
import math

import numpy as np
import jax
import jax.numpy as jnp
from jax import lax
from jax.experimental import pallas as pl
from jax.experimental.pallas import tpu as pltpu

D_MODEL = 2048
BATCH = 4
SEQ = 2048
DEPTH = 4
GRID_W = 64
CTX_LEN = 256
HEAD_DIM = 128
N_GROUPS = 4
GROUP_HEADS = 4
GROUP_WIDTH = GROUP_HEADS * HEAD_DIM
MIX_WIDTH = N_GROUPS * GROUP_WIDTH
KV_HEADS = 2
KV_WIDTH = KV_HEADS * HEAD_DIM
DIFF_DIM = HEAD_DIM // 2
WINDOW = 128
NA_WIN_H = 8
NA_WIN_W = 16
ROPE_THETA = 10000.0
EPS = 1e-6
NEG_INF = -1e30

IN_WIDTH = 3 * GROUP_WIDTH + (GROUP_WIDTH + 2 * KV_WIDTH) + 3 * GROUP_WIDTH + (GROUP_WIDTH + 2 * KV_WIDTH) + MIX_WIDTH
COL_A_Q, COL_A_K, COL_A_V = 0, 512, 1024
COL_B_Q, COL_B_K, COL_B_V = 1536, 2048, 2304
COL_C_Q, COL_C_K, COL_C_V = 2560, 3072, 3584
COL_D_Q, COL_D_K, COL_D_V = 4096, 4608, 4864
COL_GATE = 5120

N_LAT = BATCH * SEQ
N_CTX = BATCH * CTX_LEN
N_TOK = N_LAT + N_CTX
ROWS = SEQ // GRID_W

F32 = jnp.float32
BF16 = jnp.bfloat16

MIB = 1024 * 1024

TQ = 256
NQ = SEQ // TQ
IN_TM = 1024
IN_TN = 256
IN_RC = 128
OUT_TM = 512
OUT_RC = 64
MOD_TN = 768
NA_QROWS = TQ // GRID_W
NA_KROWS = 12
NA_KEYS = NA_KROWS * GRID_W
WIN_KEYS = TQ + 2 * WINDOW

_NT = (((1,), (1,)), ((), ()))


def _params(semantics, vmem_mib):
    return pltpu.CompilerParams(dimension_semantics=semantics, vmem_limit_bytes=vmem_mib * MIB)


def _silu(x):
    return x / (1.0 + jnp.exp(-x))


def _mod_kernel(cc_ref, w_ref, b_ref, o_ref):
    sc = _silu(cc_ref[...])
    o_ref[...] = jnp.dot(sc.astype(BF16), w_ref[...].astype(BF16), preferred_element_type=F32) + b_ref[...]


def _modulation(cc, w_mod, b_mod):
    n = 3 * D_MODEL
    return pl.pallas_call(
        _mod_kernel,
        grid=(DEPTH, n // MOD_TN),
        in_specs=[
            pl.BlockSpec((8, D_MODEL), lambda l, j: (0, 0)),
            pl.BlockSpec((None, D_MODEL, MOD_TN), lambda l, j: (l, 0, j)),
            pl.BlockSpec((None, 1, MOD_TN), lambda l, j: (l, 0, j)),
        ],
        out_specs=pl.BlockSpec((None, 8, MOD_TN), lambda l, j: (l, 0, j)),
        out_shape=jax.ShapeDtypeStruct((DEPTH, 8, n), F32),
        compiler_params=_params(("arbitrary", "arbitrary"), 40),
        name="modulation",
    )(cc, w_mod, b_mod.reshape(DEPTH, 1, n))


_K_PLAIN, _K_ROPE_SUB_Q, _K_ROPE_SUB, _K_ROPE_Q, _K_ROPE, _K_SCALE, _K_NORM_ROPE_Q, _K_NORM_ROPE, _K_SILU = range(9)
_SECTIONS = (
    (COL_A_Q, COL_A_K, _K_ROPE_SUB_Q), (COL_A_K, COL_A_V, _K_ROPE_SUB), (COL_A_V, COL_B_Q, _K_PLAIN),
    (COL_B_Q, COL_B_K, _K_ROPE_Q), (COL_B_K, COL_B_V, _K_ROPE), (COL_B_V, COL_C_Q, _K_PLAIN),
    (COL_C_Q, COL_C_K, _K_SCALE), (COL_C_K, COL_D_Q, _K_PLAIN),
    (COL_D_Q, COL_D_K, _K_NORM_ROPE_Q), (COL_D_K, COL_D_V, _K_NORM_ROPE), (COL_D_V, COL_GATE, _K_PLAIN),
    (COL_GATE, IN_WIDTH, _K_SILU),
)
_HEAD_SCALE = HEAD_DIM ** -0.5
_SUB_SCALE = DIFF_DIM ** -0.5


def _kind_pred(j, kind):
    pred = None
    for lo, hi, k in _SECTIONS:
        if k != kind:
            continue
        p = jnp.logical_and(j >= lo // IN_TN, j < hi // IN_TN)
        pred = p if pred is None else jnp.logical_or(pred, p)
    return pred


def _head_rms(x, g):
    return x * lax.rsqrt(jnp.mean(x * x, axis=-1, keepdims=True) + EPS) * g


def _inproj_kernel(x_ref, mod_ref, g_ref, w_ref, ch_ref, sh_ref, cs_ref, sa_ref, sb_ref, gq_ref, gk_ref,
                   o_ref, n_ref, acc_ref):
    i = pl.program_id(0)
    j = pl.program_id(1)
    n_chunks = IN_TM // IN_RC

    @pl.when(j == 0)
    def _():
        row = jnp.minimum(i // (SEQ // IN_TM), BATCH)
        shift = mod_ref[pl.ds(row, 1), 0:D_MODEL]
        scale1 = 1.0 + mod_ref[pl.ds(row, 1), D_MODEL:2 * D_MODEL]
        g = g_ref[...]

        def body(r, carry):
            sl = pl.ds(pl.multiple_of(r * IN_RC, IN_RC), IN_RC)
            x = x_ref[sl, :]
            y = x * lax.rsqrt(jnp.mean(x * x, axis=-1, keepdims=True) + EPS) * g
            n_ref[sl, :] = (y * scale1 + shift).astype(BF16)
            return carry

        lax.fori_loop(0, n_chunks, body, 0)

    acc_ref[...] = jnp.dot(n_ref[...], w_ref[...], preferred_element_type=F32)

    def epilogue(kind, fn):
        @pl.when(_kind_pred(j, kind))
        def _():
            def body(r, carry):
                sl = pl.ds(pl.multiple_of(r * IN_RC, IN_RC), IN_RC)
                for half in range(IN_TN // HEAD_DIM):
                    cols = slice(half * HEAD_DIM, (half + 1) * HEAD_DIM)
                    o_ref[sl, cols] = fn(acc_ref[sl, cols], sl).astype(BF16)
                return carry

            lax.fori_loop(0, n_chunks, body, 0)

    def rope_head(x, sl):
        return x * ch_ref[sl, :] + pltpu.roll(x, HEAD_DIM // 2, 1) * sh_ref[sl, :]

    def rope_sub(x, sl):
        return (x * cs_ref[sl, :] + pltpu.roll(x, HEAD_DIM - DIFF_DIM // 2, 1) * sa_ref[sl, :]
                + pltpu.roll(x, DIFF_DIM // 2, 1) * sb_ref[sl, :])

    epilogue(_K_PLAIN, lambda x, sl: x)
    epilogue(_K_ROPE_SUB_Q, lambda x, sl: rope_sub(x, sl) * _SUB_SCALE)
    epilogue(_K_ROPE_SUB, rope_sub)
    epilogue(_K_ROPE_Q, lambda x, sl: rope_head(x, sl) * _HEAD_SCALE)
    epilogue(_K_ROPE, rope_head)
    epilogue(_K_SCALE, lambda x, sl: x * _HEAD_SCALE)
    epilogue(_K_NORM_ROPE_Q, lambda x, sl: rope_head(_head_rms(x, gq_ref[...]), sl) * _HEAD_SCALE)
    epilogue(_K_NORM_ROPE, lambda x, sl: rope_head(_head_rms(x, gk_ref[...]), sl))
    epilogue(_K_SILU, lambda x, sl: _silu(x))


def _in_projection(hh, mod_l, pre_g, w_in_l, tables, gq, gk):
    lat_blocks = N_LAT // IN_TM
    per_seq = SEQ // IN_TM

    def tab_map(i, j):
        return (jnp.where(i < lat_blocks, i % per_seq, per_seq), 0)

    tab_spec = pl.BlockSpec((IN_TM, HEAD_DIM), tab_map)
    vec_spec = lambda n: pl.BlockSpec((1, n), lambda i, j: (0, 0))
    return pl.pallas_call(
        _inproj_kernel,
        grid=(N_TOK // IN_TM, IN_WIDTH // IN_TN),
        in_specs=[
            pl.BlockSpec((IN_TM, D_MODEL), lambda i, j: (i, 0)),
            pl.BlockSpec((8, 3 * D_MODEL), lambda i, j: (0, 0)),
            vec_spec(D_MODEL),
            pl.BlockSpec((D_MODEL, IN_TN), lambda i, j: (0, j)),
            tab_spec, tab_spec, tab_spec, tab_spec, tab_spec,
            vec_spec(HEAD_DIM), vec_spec(HEAD_DIM),
        ],
        out_specs=pl.BlockSpec((IN_TM, IN_TN), lambda i, j: (i, j)),
        out_shape=jax.ShapeDtypeStruct((N_TOK, IN_WIDTH), BF16),
        scratch_shapes=[pltpu.VMEM((IN_TM, D_MODEL), BF16), pltpu.VMEM((IN_TM, IN_TN), F32)],
        compiler_params=_params(("arbitrary", "arbitrary"), 48),
        name="in_projection",
    )(hh, mod_l, pre_g.reshape(1, D_MODEL), w_in_l, *tables, gq.reshape(1, HEAD_DIM), gk.reshape(1, HEAD_DIM))


def _rope_tables():
    t = jnp.arange(SEQ)
    row = (t // GRID_W).astype(F32)
    col = (t % GRID_W).astype(F32)

    def cos_sin(dim):
        n_freq = dim // 4
        inv = ROPE_THETA ** (-jnp.arange(n_freq, dtype=F32) / n_freq)
        ang = jnp.concatenate([row[:, None] * inv, col[:, None] * inv], axis=-1)
        return jnp.cos(ang), jnp.sin(ang)

    ch, sh = cos_sin(HEAD_DIM)
    cs, ss = cos_sin(DIFF_DIM)
    zs = jnp.zeros_like(ss)
    tabs = [
        jnp.concatenate([ch, ch], axis=-1),
        jnp.concatenate([-sh, sh], axis=-1),
        jnp.concatenate([cs, cs, cs, cs], axis=-1),
        jnp.concatenate([-ss, zs, -ss, zs], axis=-1),
        jnp.concatenate([zs, ss, zs, ss], axis=-1),
    ]
    ident = [jnp.ones, jnp.zeros, jnp.ones, jnp.zeros, jnp.zeros]
    return [jnp.concatenate([tb, f((IN_TM, HEAD_DIM), F32)], axis=0) for tb, f in zip(tabs, ident)]


def _softmax_pv(q, segs, extra=None):
    scores = []
    m = extra
    for k, v, fix in segs:
        s = lax.dot_general(q, k, _NT, preferred_element_type=F32)
        if fix is not None:
            s = fix(s)
        scores.append(s)
        sm = jnp.max(s, axis=-1, keepdims=True)
        m = sm if m is None else jnp.maximum(m, sm)
    l = None if extra is None else jnp.exp(extra - m)
    o = None
    for s, (k, v, fix) in zip(scores, segs):
        e = jnp.exp(s - m)
        ls = jnp.sum(e, axis=-1, keepdims=True)
        l = ls if l is None else l + ls
        pv = jnp.dot(e.astype(BF16), v, preferred_element_type=F32)
        o = pv if o is None else o + pv
    return o / l


def _is_ctx_step():
    return pl.program_id(2) == NQ


def _q_row_map(b, h, i):
    return jnp.where(i < NQ, b * NQ + i, BATCH * NQ + b)


def _diff_kernel(q_ref, kl_ref, kc_ref, vl_ref, vc_ref, lq1_ref, lk1_ref, lq2_ref, lk2_ref, g_ref, o_ref, *,
                 lambda_init, with_ctx):
    lam = (jnp.exp(jnp.sum(lq1_ref[...] * lk1_ref[...], axis=-1, keepdims=True))
           - jnp.exp(jnp.sum(lq2_ref[...] * lk2_ref[...], axis=-1, keepdims=True)) + lambda_init)

    def run(segs):
        q = q_ref[...]
        lane = lax.broadcasted_iota(jnp.int32, q.shape, 1)
        zero = jnp.zeros_like(q)
        qq = jnp.concatenate([jnp.where(lane < DIFF_DIM, q, zero), jnp.where(lane >= DIFF_DIM, q, zero)], axis=0)
        o = _softmax_pv(qq, segs)
        o = o[:TQ] - lam * o[TQ:]
        o_ref[...] = (_head_rms(o, g_ref[...]) * (1.0 - lambda_init)).astype(BF16)

    ctx_seg = (kc_ref[...], vc_ref[...], None)
    if with_ctx:
        @pl.when(_is_ctx_step())
        def _():
            run([ctx_seg])

    @pl.when(jnp.logical_not(_is_ctx_step()))
    def _():
        run([(kl_ref[...], vl_ref[...], None), ctx_seg])


def _window_kernel(sink_ref, q_ref, kl_ref, kc_ref, vl_ref, vc_ref, o_ref, *, with_ctx):
    kvh = pl.program_id(1)
    i = pl.program_id(2)
    g = GROUP_HEADS // KV_HEADS
    rows = lax.broadcasted_iota(jnp.int32, (g * TQ, 1), 0)
    sink = jnp.where(rows < TQ, sink_ref[kvh * g], sink_ref[kvh * g + 1]).astype(F32)

    def stacked_q():
        return jnp.concatenate([q_ref[:, h * HEAD_DIM:(h + 1) * HEAD_DIM] for h in range(g)], axis=0)

    def store(o):
        for h in range(g):
            o_ref[:, h * HEAD_DIM:(h + 1) * HEAD_DIM] = o[h * TQ:(h + 1) * TQ].astype(BF16)

    ctx_seg = (kc_ref[...], vc_ref[...], None)
    if with_ctx:
        @pl.when(_is_ctx_step())
        def _():
            store(_softmax_pv(stacked_q(), [ctx_seg], extra=sink))

    @pl.when(jnp.logical_not(_is_ctx_step()))
    def _():
        q0 = i * TQ
        ws = pl.multiple_of(jnp.clip(q0 - WINDOW, 0, SEQ - WIN_KEYS), WINDOW)
        kw = kl_ref[pl.ds(ws, WIN_KEYS), :]
        vw = vl_ref[pl.ds(ws, WIN_KEYS), :]
        qpos = q0 + (lax.broadcasted_iota(jnp.int32, (g * TQ, WIN_KEYS), 0) & (TQ - 1))
        kpos = ws + lax.broadcasted_iota(jnp.int32, (g * TQ, WIN_KEYS), 1)
        valid = jnp.abs(kpos - qpos) <= WINDOW
        band = lambda s: jnp.where(valid, s, NEG_INF)
        store(_softmax_pv(stacked_q(), [(kw, vw, band), ctx_seg], extra=sink))


def _neigh_kernel(q_ref, kl_ref, kc_ref, vl_ref, vc_ref, bias_ref, o_ref, *, with_ctx):
    i = pl.program_id(2)
    ctx_seg = (kc_ref[...], vc_ref[...], None)
    if with_ctx:
        @pl.when(_is_ctx_step())
        def _():
            o_ref[...] = _softmax_pv(q_ref[...], [ctx_seg]).astype(BF16)

    @pl.when(jnp.logical_not(_is_ctx_step()))
    def _():
        wr0 = jnp.clip(i * NA_QROWS - NA_WIN_H // 2, 0, ROWS - NA_KROWS)
        ws = pl.multiple_of(wr0 * GRID_W, GRID_W)
        kw = kl_ref[pl.ds(ws, NA_KEYS), :]
        vw = vl_ref[pl.ds(ws, NA_KEYS), :]
        add_bias = lambda s: s + bias_ref[...]
        o_ref[...] = _softmax_pv(q_ref[...], [(kw, vw, add_bias), ctx_seg]).astype(BF16)


def _qknorm_kernel(q_ref, kl_ref, kc_ref, vl_ref, vc_ref, o_ref, *, with_ctx):
    g = GROUP_HEADS // KV_HEADS

    def run(segs):
        qq = jnp.concatenate([q_ref[:, h * HEAD_DIM:(h + 1) * HEAD_DIM] for h in range(g)], axis=0)
        o = _softmax_pv(qq, segs)
        for h in range(g):
            o_ref[:, h * HEAD_DIM:(h + 1) * HEAD_DIM] = o[h * TQ:(h + 1) * TQ].astype(BF16)

    ctx_seg = (kc_ref[...], vc_ref[...], None)
    if with_ctx:
        @pl.when(_is_ctx_step())
        def _():
            run([ctx_seg])

    @pl.when(jnp.logical_not(_is_ctx_step()))
    def _():
        run([(kl_ref[...], vl_ref[...], None), ctx_seg])


def _attention_call(kernel, name, p, *, q_col, k_col, v_col, q_heads, n_kv, with_ctx, pre=(), post=(), vmem_mib=48):
    qw = q_heads * HEAD_DIM
    ctx_blk0 = N_LAT // CTX_LEN
    steps = NQ + 1 if with_ctx else NQ
    q_spec = pl.BlockSpec((TQ, qw), lambda b, h, i: (_q_row_map(b, h, i), q_col // qw + h))
    kv_lat = lambda col: pl.BlockSpec((SEQ, HEAD_DIM), lambda b, h, i: (b, col // HEAD_DIM + h))
    kv_ctx = lambda col: pl.BlockSpec((CTX_LEN, HEAD_DIM), lambda b, h, i: (ctx_blk0 + b, col // HEAD_DIM + h))
    pre_specs, pre_args = zip(*pre) if pre else ((), ())
    post_specs, post_args = zip(*post) if post else ((), ())
    rows = N_TOK if with_ctx else N_LAT
    return pl.pallas_call(
        kernel,
        grid=(BATCH, n_kv, steps),
        in_specs=[*pre_specs, q_spec, kv_lat(k_col), kv_ctx(k_col), kv_lat(v_col), kv_ctx(v_col), *post_specs],
        out_specs=pl.BlockSpec((TQ, qw), lambda b, h, i: (_q_row_map(b, h, i), h)),
        out_shape=jax.ShapeDtypeStruct((rows, GROUP_WIDTH), BF16),
        compiler_params=_params(("arbitrary", "arbitrary", "arbitrary"), vmem_mib),
        name=name,
    )(*pre_args, p, p, p, p, p, *post_args)


def _const_spec(shape):
    return pl.BlockSpec(shape, lambda b, h, i: (0,) * len(shape))


def _neigh_bias(rpb):
    blocks = []
    t = np.arange(TQ)
    u = np.arange(NA_KEYS)
    for blk in (0, 1, NQ - 1):
        r0 = blk * NA_QROWS
        wr0 = int(np.clip(r0 - NA_WIN_H // 2, 0, ROWS - NA_KROWS))
        rq = (r0 + t // GRID_W)[:, None]
        cq = (t % GRID_W)[:, None]
        rk = (wr0 + u // GRID_W)[None, :]
        ck = (u % GRID_W)[None, :]
        rstart = np.clip(rq - NA_WIN_H // 2, 0, ROWS - NA_WIN_H)
        cstart = np.clip(cq - NA_WIN_W // 2, 0, GRID_W - NA_WIN_W)
        valid = (rk >= rstart) & (rk < rstart + NA_WIN_H) & (ck >= cstart) & (ck < cstart + NA_WIN_W)
        rel_r = np.clip(rk - rq + NA_WIN_H - 1, 0, 2 * NA_WIN_H - 2)
        rel_c = np.clip(ck - cq + NA_WIN_W - 1, 0, 2 * NA_WIN_W - 2)
        b = rpb.astype(F32)[:, :, rel_r, rel_c]
        blocks.append(jnp.where(valid, b, NEG_INF))
    return jnp.stack(blocks, axis=2)


def _outproj_kernel(oa_ref, ob_ref, oc_ref, od_ref, ga_ref, gb_ref, gc_ref, gd_ref, h_ref, mod_ref, pg_ref, w_ref,
                    out_ref, a_ref, y_ref):
    i = pl.program_id(0)
    row = jnp.minimum(i // (SEQ // OUT_TM), BATCH)
    gate = mod_ref[pl.ds(row, 1), 2 * D_MODEL:3 * D_MODEL]
    for n, (o, g) in enumerate(((oa_ref, ga_ref), (ob_ref, gb_ref), (oc_ref, gc_ref), (od_ref, gd_ref))):
        a_ref[:, n * GROUP_WIDTH:(n + 1) * GROUP_WIDTH] = o[...] * g[...]
    y_ref[...] = jnp.dot(a_ref[...], w_ref[...], preferred_element_type=F32)
    pg = pg_ref[...]

    def body(r, carry):
        sl = pl.ds(pl.multiple_of(r * OUT_RC, OUT_RC), OUT_RC)
        y = y_ref[sl, :]
        yn = y * lax.rsqrt(jnp.mean(y * y, axis=-1, keepdims=True) + EPS) * pg
        out_ref[sl, :] = h_ref[sl, :] + gate * yn
        return carry

    lax.fori_loop(0, OUT_TM // OUT_RC, body, 0)


def _out_projection(outs, p, hh, mod_l, post_g, w_out_l, with_ctx):
    rows = N_TOK if with_ctx else N_LAT
    o_spec = pl.BlockSpec((OUT_TM, GROUP_WIDTH), lambda i: (i, 0))
    gate_spec = lambda n: pl.BlockSpec((OUT_TM, GROUP_WIDTH), lambda i: (i, COL_GATE // GROUP_WIDTH + n))
    return pl.pallas_call(
        _outproj_kernel,
        grid=(rows // OUT_TM,),
        in_specs=[
            o_spec, o_spec, o_spec, o_spec,
            gate_spec(0), gate_spec(1), gate_spec(2), gate_spec(3),
            pl.BlockSpec((OUT_TM, D_MODEL), lambda i: (i, 0)),
            pl.BlockSpec((8, 3 * D_MODEL), lambda i: (0, 0)),
            pl.BlockSpec((1, D_MODEL), lambda i: (0, 0)),
            pl.BlockSpec((MIX_WIDTH, D_MODEL), lambda i: (0, 0)),
        ],
        out_specs=pl.BlockSpec((OUT_TM, D_MODEL), lambda i: (i, 0)),
        out_shape=jax.ShapeDtypeStruct((rows, D_MODEL), F32),
        scratch_shapes=[pltpu.VMEM((OUT_TM, MIX_WIDTH), BF16), pltpu.VMEM((OUT_TM, D_MODEL), F32)],
        compiler_params=_params(("arbitrary",), 56),
        name="out_projection",
    )(*outs, p, p, p, p, hh, mod_l, post_g.reshape(1, D_MODEL), w_out_l)


def kernel(x, c, ctx, c_ctx, w_mod, b_mod, pre_norm_g, w_in, diff_lambda_q1, diff_lambda_k1, diff_lambda_q2,
           diff_lambda_k2, diff_subln_g, win_sink, na_rpb, qk_q_norm_g, qk_k_norm_g, w_out, post_norm_g):
    assert x.shape == (BATCH, SEQ, D_MODEL) and ctx.shape == (BATCH, CTX_LEN, D_MODEL)
    cc = jnp.concatenate([c, c_ctx[None], jnp.zeros((8 - BATCH - 1, D_MODEL), F32)], axis=0)
    mod = _modulation(cc, w_mod, b_mod)
    tables = _rope_tables()
    na_bias = _neigh_bias(na_rpb)
    w_in_b = w_in.astype(BF16)
    w_out_b = w_out.astype(BF16)
    hh = jnp.concatenate([x.reshape(N_LAT, D_MODEL), ctx.reshape(N_CTX, D_MODEL)], axis=0)

    vec = lambda a, n: a.reshape(1, n)
    for l in range(DEPTH):
        with_ctx = l < DEPTH - 1
        lambda_init = 0.8 - 0.6 * math.exp(-0.3 * l)
        p = _in_projection(hh, mod[l], pre_norm_g[l], w_in_b[l], tables, qk_q_norm_g[l], qk_k_norm_g[l])

        lam_vecs = [(_const_spec((1, DIFF_DIM)), vec(a[l], DIFF_DIM))
                    for a in (diff_lambda_q1, diff_lambda_k1, diff_lambda_q2, diff_lambda_k2)]
        oa = _attention_call(
            functools.partial(_diff_kernel, lambda_init=lambda_init, with_ctx=with_ctx), "diff_attention", p,
            q_col=COL_A_Q, k_col=COL_A_K, v_col=COL_A_V, q_heads=1, n_kv=GROUP_HEADS, with_ctx=with_ctx,
            post=lam_vecs + [(_const_spec((1, HEAD_DIM)), vec(diff_subln_g[l], HEAD_DIM))])
        ob = _attention_call(
            functools.partial(_window_kernel, with_ctx=with_ctx), "window_attention", p,
            q_col=COL_B_Q, k_col=COL_B_K, v_col=COL_B_V, q_heads=GROUP_HEADS // KV_HEADS, n_kv=KV_HEADS,
            with_ctx=with_ctx, pre=[(pl.BlockSpec(memory_space=pltpu.SMEM), win_sink[l])])
        bias_spec = pl.BlockSpec(
            (None, None, TQ, NA_KEYS),
            lambda b, h, i: (h, jnp.where(i == 0, 0, jnp.where(i >= NQ - 1, 2, 1)), 0, 0))
        on = _attention_call(
            functools.partial(_neigh_kernel, with_ctx=with_ctx), "neighborhood_attention", p,
            q_col=COL_C_Q, k_col=COL_C_K, v_col=COL_C_V, q_heads=1, n_kv=GROUP_HEADS, with_ctx=with_ctx,
            post=[(bias_spec, na_bias[l])])
        od = _attention_call(
            functools.partial(_qknorm_kernel, with_ctx=with_ctx), "qknorm_attention", p,
            q_col=COL_D_Q, k_col=COL_D_K, v_col=COL_D_V, q_heads=GROUP_HEADS // KV_HEADS, n_kv=KV_HEADS,
            with_ctx=with_ctx)

        hh = _out_projection((oa, ob, on, od), p, hh, mod[l], post_norm_g[l], w_out_b[l], with_ctx)
    return hh.reshape(BATCH, SEQ, D_MODEL)
```

```python
import functools
import math

import numpy as np
import jax
import jax.numpy as jnp
from jax import lax
from jax.experimental import pallas as pl
from jax.experimental.pallas import tpu as pltpu

D_MODEL = 2048
BATCH = 4
SEQ = 2048
DEPTH = 4
GRID_W = 64
CTX_LEN = 256
HEAD_DIM = 128
N_GROUPS = 4
GROUP_HEADS = 4
GROUP_WIDTH = GROUP_HEADS * HEAD_DIM
MIX_WIDTH = N_GROUPS * GROUP_WIDTH
KV_HEADS = 2
KV_WIDTH = KV_HEADS * HEAD_DIM
DIFF_DIM = HEAD_DIM // 2
WINDOW = 128
NA_WIN_H = 8
NA_WIN_W = 16
ROPE_THETA = 10000.0
EPS = 1e-6
NEG_INF = -1e30

IN_WIDTH = 3 * GROUP_WIDTH + (GROUP_WIDTH + 2 * KV_WIDTH) + 3 * GROUP_WIDTH + (GROUP_WIDTH + 2 * KV_WIDTH) + MIX_WIDTH
COL_A_Q, COL_A_K, COL_A_V = 0, 512, 1024
COL_B_Q, COL_B_K, COL_B_V = 1536, 2048, 2304
COL_C_Q, COL_C_K, COL_C_V = 2560, 3072, 3584
COL_D_Q, COL_D_K, COL_D_V = 4096, 4608, 4864
COL_GATE = 5120

N_LAT = BATCH * SEQ
N_CTX = BATCH * CTX_LEN
N_TOK = N_LAT + N_CTX
ROWS = SEQ // GRID_W

F32 = jnp.float32
BF16 = jnp.bfloat16

MIB = 1024 * 1024

TQ = 256
NQ = SEQ // TQ
IN_TM = 1024
IN_TN = 256
IN_RC = 128
IN_NORM_RC = 128
OUT_TM = 512
OUT_RC = 64
MOD_TN = 768
NA_QROWS = TQ // GRID_W
NA_KROWS = 12
NA_KEYS = NA_KROWS * GRID_W
WIN_KEYS = TQ + 2 * WINDOW

_NT = (((1,), (1,)), ((), ()))


def _params(semantics, vmem_mib):
    return pltpu.CompilerParams(dimension_semantics=semantics, vmem_limit_bytes=vmem_mib * MIB)


def _silu(x):
    return x / (1.0 + jnp.exp(-x))


def _mod_kernel(cc_ref, w_ref, b_ref, o_ref):
    sc = _silu(cc_ref[...])
    o_ref[...] = jnp.dot(sc.astype(BF16), w_ref[...].astype(BF16), preferred_element_type=F32) + b_ref[...]


def _modulation(cc, w_mod, b_mod):
    n = 3 * D_MODEL
    return pl.pallas_call(
        _mod_kernel,
        grid=(DEPTH, n // MOD_TN),
        in_specs=[
            pl.BlockSpec((8, D_MODEL), lambda l, j: (0, 0)),
            pl.BlockSpec((None, D_MODEL, MOD_TN), lambda l, j: (l, 0, j)),
            pl.BlockSpec((None, 1, MOD_TN), lambda l, j: (l, 0, j)),
        ],
        out_specs=pl.BlockSpec((None, 8, MOD_TN), lambda l, j: (l, 0, j)),
        out_shape=jax.ShapeDtypeStruct((DEPTH, 8, n), F32),
        compiler_params=_params(("arbitrary", "arbitrary"), 40),
        name="modulation",
    )(cc, w_mod, b_mod.reshape(DEPTH, 1, n))


_K_PLAIN, _K_ROPE_SUB_Q, _K_ROPE_SUB, _K_ROPE_Q, _K_ROPE, _K_SCALE, _K_NORM_ROPE_Q, _K_NORM_ROPE, _K_SILU = range(9)
_SECTIONS = (
    (COL_A_Q, COL_A_K, _K_ROPE_SUB_Q), (COL_A_K, COL_A_V, _K_ROPE_SUB), (COL_A_V, COL_B_Q, _K_PLAIN),
    (COL_B_Q, COL_B_K, _K_ROPE_Q), (COL_B_K, COL_B_V, _K_ROPE), (COL_B_V, COL_C_Q, _K_PLAIN),
    (COL_C_Q, COL_C_K, _K_SCALE), (COL_C_K, COL_D_Q, _K_PLAIN),
    (COL_D_Q, COL_D_K, _K_NORM_ROPE_Q), (COL_D_K, COL_D_V, _K_NORM_ROPE), (COL_D_V, COL_GATE, _K_PLAIN),
    (COL_GATE, IN_WIDTH, _K_SILU),
)
_HEAD_SCALE = HEAD_DIM ** -0.5
_SUB_SCALE = DIFF_DIM ** -0.5


def _kind_pred(j, kind):
    pred = None
    for lo, hi, k in _SECTIONS:
        if k != kind:
            continue
        p = jnp.logical_and(j >= lo // IN_TN, j < hi // IN_TN)
        pred = p if pred is None else jnp.logical_or(pred, p)
    return pred


def _head_rms(x, g):
    return x * lax.rsqrt(jnp.mean(x * x, axis=-1, keepdims=True) + EPS) * g


def _inproj_kernel(x_ref, mod_ref, g_ref, w_ref, ch_ref, sh_ref, cs_ref, sa_ref, sb_ref, gq_ref, gk_ref,
                   o_ref, n_ref, acc_ref):
    i = pl.program_id(0)
    j = pl.program_id(1)
    n_chunks = IN_TM // IN_RC

    @pl.when(j == 0)
    def _():
        row = jnp.minimum(i // (SEQ // IN_TM), BATCH)
        shift = mod_ref[pl.ds(row, 1), 0:D_MODEL]
        scale1 = 1.0 + mod_ref[pl.ds(row, 1), D_MODEL:2 * D_MODEL]
        g = g_ref[...]

        def body(r, carry):
            sl = pl.ds(pl.multiple_of(r * IN_NORM_RC, IN_NORM_RC), IN_NORM_RC)
            inv = lax.rsqrt(jnp.mean(jnp.square(x_ref[sl, :]), axis=-1, keepdims=True) + EPS)
            n_ref[sl, :] = (x_ref[sl, :] * inv * g * scale1 + shift).astype(BF16)
            return carry

        lax.fori_loop(0, IN_TM // IN_NORM_RC, body, 0)

    acc_ref[...] = jnp.dot(n_ref[...], w_ref[...], preferred_element_type=F32)

    def epilogue(kind, fn, unroll=4):
        @pl.when(_kind_pred(j, kind))
        def _():
            def body(r, carry):
                sl = pl.ds(pl.multiple_of(r * IN_RC, IN_RC), IN_RC)
                for half in range(IN_TN // HEAD_DIM):
                    cols = slice(half * HEAD_DIM, (half + 1) * HEAD_DIM)
                    o_ref[sl, cols] = fn(acc_ref[sl, cols], sl).astype(BF16)
                return carry

            lax.fori_loop(0, n_chunks, body, 0, unroll=unroll)

    def rope_head(x, sl):
        return x * ch_ref[sl, :] + pltpu.roll(x, HEAD_DIM // 2, 1) * sh_ref[sl, :]

    def rope_sub(x, sl):
        return (x * cs_ref[sl, :] + pltpu.roll(x, HEAD_DIM - DIFF_DIM // 2, 1) * sa_ref[sl, :]
                + pltpu.roll(x, DIFF_DIM // 2, 1) * sb_ref[sl, :])

    epilogue(_K_PLAIN, lambda x, sl: x)
    epilogue(_K_ROPE_SUB_Q, lambda x, sl: rope_sub(x, sl) * _SUB_SCALE)
    epilogue(_K_ROPE_SUB, rope_sub)
    epilogue(_K_ROPE_Q, lambda x, sl: rope_head(x, sl) * _HEAD_SCALE)
    epilogue(_K_ROPE, rope_head)
    epilogue(_K_SCALE, lambda x, sl: x * _HEAD_SCALE)
    epilogue(_K_NORM_ROPE_Q, lambda x, sl: rope_head(_head_rms(x, gq_ref[...]), sl) * _HEAD_SCALE, unroll=2)
    epilogue(_K_NORM_ROPE, lambda x, sl: rope_head(_head_rms(x, gk_ref[...]), sl), unroll=2)
    epilogue(_K_SILU, lambda x, sl: _silu(x))


def _in_projection(l, hh, mod, pre_g, w_in, tables, gq, gk):
    lat_blocks = N_LAT // IN_TM
    per_seq = SEQ // IN_TM

    def tab_map(i, j):
        return (jnp.where(i < lat_blocks, i % per_seq, per_seq), 0)

    tab_spec = pl.BlockSpec((IN_TM, HEAD_DIM), tab_map)
    vec_spec = lambda n: pl.BlockSpec((1, n), lambda i, j: (0, 0))
    return pl.pallas_call(
        _inproj_kernel,
        grid=(N_TOK // IN_TM, IN_WIDTH // IN_TN),
        in_specs=[
            pl.BlockSpec((IN_TM, D_MODEL), lambda i, j: (i, 0)),
            pl.BlockSpec((None, 8, 3 * D_MODEL), lambda i, j: (l, 0, 0)),
            vec_spec(D_MODEL),
            pl.BlockSpec((None, D_MODEL, IN_TN), lambda i, j: (l, 0, j)),
            tab_spec, tab_spec, tab_spec, tab_spec, tab_spec,
            vec_spec(HEAD_DIM), vec_spec(HEAD_DIM),
        ],
        out_specs=pl.BlockSpec((IN_TM, IN_TN), lambda i, j: (i, j)),
        out_shape=jax.ShapeDtypeStruct((N_TOK, IN_WIDTH), BF16),
        scratch_shapes=[pltpu.VMEM((IN_TM, D_MODEL), BF16), pltpu.VMEM((IN_TM, IN_TN), F32)],
        compiler_params=_params(("arbitrary", "arbitrary"), 48),
        name="in_projection",
    )(hh, mod, pre_g.reshape(1, D_MODEL), w_in, *tables, gq.reshape(1, HEAD_DIM), gk.reshape(1, HEAD_DIM))


def _rope_tables():
    t = jnp.arange(SEQ)
    row = (t // GRID_W).astype(F32)
    col = (t % GRID_W).astype(F32)

    def cos_sin(dim):
        n_freq = dim // 4
        inv = ROPE_THETA ** (-jnp.arange(n_freq, dtype=F32) / n_freq)
        ang = jnp.concatenate([row[:, None] * inv, col[:, None] * inv], axis=-1)
        return jnp.cos(ang), jnp.sin(ang)

    ch, sh = cos_sin(HEAD_DIM)
    cs, ss = cos_sin(DIFF_DIM)
    zs = jnp.zeros_like(ss)
    tabs = [
        jnp.concatenate([ch, ch], axis=-1),
        jnp.concatenate([-sh, sh], axis=-1),
        jnp.concatenate([cs, cs, cs, cs], axis=-1),
        jnp.concatenate([-ss, zs, -ss, zs], axis=-1),
        jnp.concatenate([zs, ss, zs, ss], axis=-1),
    ]
    ident = [jnp.ones, jnp.zeros, jnp.ones, jnp.zeros, jnp.zeros]
    return [jnp.concatenate([tb, f((IN_TM, HEAD_DIM), F32)], axis=0) for tb, f in zip(tabs, ident)]


def _softmax_pv(q, segs, extra=None):
    scores = []
    m = extra
    for k, v, fix in segs:
        s = lax.dot_general(q, k, _NT, preferred_element_type=F32)
        if fix is not None:
            s = fix(s)
        scores.append(s)
        sm = jnp.max(s, axis=-1, keepdims=True)
        m = sm if m is None else jnp.maximum(m, sm)
    l = None if extra is None else jnp.exp(extra - m)
    o = None
    for s, (k, v, fix) in zip(scores, segs):
        e = jnp.exp(s - m)
        ls = jnp.sum(e, axis=-1, keepdims=True)
        l = ls if l is None else l + ls
        pv = jnp.dot(e.astype(BF16), v, preferred_element_type=F32)
        o = pv if o is None else o + pv
    return o / l


def _is_ctx_step():
    return pl.program_id(2) == NQ


def _q_row_map(b, h, i):
    return jnp.where(i < NQ, b * NQ + i, BATCH * NQ + b)


def _diff_kernel(q_ref, kl_ref, kc_ref, vl_ref, vc_ref, lq1_ref, lk1_ref, lq2_ref, lk2_ref, g_ref, o_ref, *,
                 lambda_init, with_ctx):
    lam = (jnp.exp(jnp.sum(lq1_ref[...] * lk1_ref[...], axis=-1, keepdims=True))
           - jnp.exp(jnp.sum(lq2_ref[...] * lk2_ref[...], axis=-1, keepdims=True)) + lambda_init)

    def run(segs):
        q = q_ref[...]
        lane = lax.broadcasted_iota(jnp.int32, q.shape, 1)
        zero = jnp.zeros_like(q)
        qq = jnp.concatenate([jnp.where(lane < DIFF_DIM, q, zero), jnp.where(lane >= DIFF_DIM, q, zero)], axis=0)
        o = _softmax_pv(qq, segs)
        o = o[:TQ] - lam * o[TQ:]
        o_ref[...] = (_head_rms(o, g_ref[...]) * (1.0 - lambda_init)).astype(BF16)

    ctx_seg = (kc_ref[...], vc_ref[...], None)
    if with_ctx:
        @pl.when(_is_ctx_step())
        def _():
            run([ctx_seg])

    @pl.when(jnp.logical_not(_is_ctx_step()))
    def _():
        run([(kl_ref[...], vl_ref[...], None), ctx_seg])


def _window_kernel(sink_ref, q_ref, kl_ref, kc_ref, vl_ref, vc_ref, o_ref, *, with_ctx):
    kvh = pl.program_id(1)
    i = pl.program_id(2)
    g = GROUP_HEADS // KV_HEADS
    rows = lax.broadcasted_iota(jnp.int32, (g * TQ, 1), 0)
    sink = jnp.where(rows < TQ, sink_ref[kvh * g], sink_ref[kvh * g + 1]).astype(F32)

    def stacked_q():
        return jnp.concatenate([q_ref[:, h * HEAD_DIM:(h + 1) * HEAD_DIM] for h in range(g)], axis=0)

    def store(o):
        for h in range(g):
            o_ref[:, h * HEAD_DIM:(h + 1) * HEAD_DIM] = o[h * TQ:(h + 1) * TQ].astype(BF16)

    ctx_seg = (kc_ref[...], vc_ref[...], None)
    if with_ctx:
        @pl.when(_is_ctx_step())
        def _():
            store(_softmax_pv(stacked_q(), [ctx_seg], extra=sink))

    @pl.when(jnp.logical_not(_is_ctx_step()))
    def _():
        q0 = i * TQ
        ws = pl.multiple_of(jnp.clip(q0 - WINDOW, 0, SEQ - WIN_KEYS), WINDOW)
        kw = kl_ref[pl.ds(ws, WIN_KEYS), :]
        vw = vl_ref[pl.ds(ws, WIN_KEYS), :]
        qpos = q0 + (lax.broadcasted_iota(jnp.int32, (g * TQ, WIN_KEYS), 0) & (TQ - 1))
        kpos = ws + lax.broadcasted_iota(jnp.int32, (g * TQ, WIN_KEYS), 1)
        valid = jnp.abs(kpos - qpos) <= WINDOW
        band = lambda s: jnp.where(valid, s, NEG_INF)
        store(_softmax_pv(stacked_q(), [(kw, vw, band), ctx_seg], extra=sink))


def _neigh_kernel(q_ref, kl_ref, kc_ref, vl_ref, vc_ref, bias_ref, o_ref, *, with_ctx):
    i = pl.program_id(2)
    ctx_seg = (kc_ref[...], vc_ref[...], None)
    if with_ctx:
        @pl.when(_is_ctx_step())
        def _():
            o_ref[...] = _softmax_pv(q_ref[...], [ctx_seg]).astype(BF16)

    @pl.when(jnp.logical_not(_is_ctx_step()))
    def _():
        wr0 = jnp.clip(i * NA_QROWS - NA_WIN_H // 2, 0, ROWS - NA_KROWS)
        ws = pl.multiple_of(wr0 * GRID_W, GRID_W)
        kw = kl_ref[pl.ds(ws, NA_KEYS), :]
        vw = vl_ref[pl.ds(ws, NA_KEYS), :]
        add_bias = lambda s: s + bias_ref[...]
        o_ref[...] = _softmax_pv(q_ref[...], [(kw, vw, add_bias), ctx_seg]).astype(BF16)


def _qknorm_kernel(q_ref, kl_ref, kc_ref, vl_ref, vc_ref, o_ref, *, with_ctx):
    g = GROUP_HEADS // KV_HEADS

    def run(segs):
        qq = jnp.concatenate([q_ref[:, h * HEAD_DIM:(h + 1) * HEAD_DIM] for h in range(g)], axis=0)
        o = _softmax_pv(qq, segs)
        for h in range(g):
            o_ref[:, h * HEAD_DIM:(h + 1) * HEAD_DIM] = o[h * TQ:(h + 1) * TQ].astype(BF16)

    ctx_seg = (kc_ref[...], vc_ref[...], None)
    if with_ctx:
        @pl.when(_is_ctx_step())
        def _():
            run([ctx_seg])

    @pl.when(jnp.logical_not(_is_ctx_step()))
    def _():
        run([(kl_ref[...], vl_ref[...], None), ctx_seg])


def _attention_call(kernel, name, p, *, q_col, k_col, v_col, q_heads, n_kv, with_ctx, pre=(), post=(), vmem_mib=48):
    qw = q_heads * HEAD_DIM
    ctx_blk0 = N_LAT // CTX_LEN
    steps = NQ + 1 if with_ctx else NQ
    q_spec = pl.BlockSpec((TQ, qw), lambda b, h, i: (_q_row_map(b, h, i), q_col // qw + h))
    kv_lat = lambda col: pl.BlockSpec((SEQ, HEAD_DIM), lambda b, h, i: (b, col // HEAD_DIM + h))
    kv_ctx = lambda col: pl.BlockSpec((CTX_LEN, HEAD_DIM), lambda b, h, i: (ctx_blk0 + b, col // HEAD_DIM + h))
    pre_specs, pre_args = zip(*pre) if pre else ((), ())
    post_specs, post_args = zip(*post) if post else ((), ())
    rows = N_TOK if with_ctx else N_LAT
    return pl.pallas_call(
        kernel,
        grid=(BATCH, n_kv, steps),
        in_specs=[*pre_specs, q_spec, kv_lat(k_col), kv_ctx(k_col), kv_lat(v_col), kv_ctx(v_col), *post_specs],
        out_specs=pl.BlockSpec((TQ, qw), lambda b, h, i: (_q_row_map(b, h, i), h)),
        out_shape=jax.ShapeDtypeStruct((rows, GROUP_WIDTH), BF16),
        compiler_params=_params(("arbitrary", "arbitrary", "arbitrary"), vmem_mib),
        name=name,
    )(*pre_args, p, p, p, p, p, *post_args)


def _const_spec(shape):
    return pl.BlockSpec(shape, lambda b, h, i: (0,) * len(shape))


def _neigh_bias(rpb):
    n_rel_r, n_rel_c = 2 * NA_WIN_H - 1, 2 * NA_WIN_W - 1
    cq = np.arange(GRID_W)[:, None]
    ck = np.arange(GRID_W)[None, :]
    cstart = np.clip(cq - NA_WIN_W // 2, 0, GRID_W - NA_WIN_W)
    col_ok = (ck >= cstart) & (ck < cstart + NA_WIN_W)
    rel_c = np.clip(ck - cq + NA_WIN_W - 1, 0, n_rel_c - 1)
    pick_c = (rel_c[None] == np.arange(n_rel_c)[:, None, None]).astype(np.float32)
    strip = jnp.einsum("dhrc,cqk->dhqrk", rpb.astype(F32), pick_c, precision=lax.Precision.HIGHEST)
    strip = jnp.where(col_ok[:, None, :], strip, NEG_INF).reshape(DEPTH, GROUP_HEADS, GRID_W, n_rel_r * GRID_W)

    def neg(n_rows):
        return jnp.full((DEPTH, GROUP_HEADS, GRID_W, n_rows * GRID_W), NEG_INF, F32)

    blocks = []
    for blk in (0, 1, NQ - 1):
        r0 = blk * NA_QROWS
        wr0 = int(np.clip(r0 - NA_WIN_H // 2, 0, ROWS - NA_KROWS))
        q_rows = []
        for rq in range(r0, r0 + NA_QROWS):
            rstart = int(np.clip(rq - NA_WIN_H // 2, 0, ROWS - NA_WIN_H))
            before = rstart - wr0
            after = NA_KROWS - NA_WIN_H - before
            rel0 = rstart - rq + NA_WIN_H - 1
            pieces = [neg(before)] if before else []
            pieces.append(strip[..., rel0 * GRID_W:(rel0 + NA_WIN_H) * GRID_W])
            if after:
                pieces.append(neg(after))
            q_rows.append(jnp.concatenate(pieces, axis=-1))
        blocks.append(jnp.concatenate(q_rows, axis=2))
    return jnp.stack(blocks, axis=2)


def _outproj_kernel(oa_ref, ob_ref, oc_ref, od_ref, ga_ref, gb_ref, gc_ref, gd_ref, h_ref, mod_ref, pg_ref, w_ref,
                    out_ref, a_ref, y_ref):
    i = pl.program_id(0)
    row = jnp.minimum(i // (SEQ // OUT_TM), BATCH)
    gate = mod_ref[pl.ds(row, 1), 2 * D_MODEL:3 * D_MODEL]
    for n, (o, g) in enumerate(((oa_ref, ga_ref), (ob_ref, gb_ref), (oc_ref, gc_ref), (od_ref, gd_ref))):
        a_ref[:, n * GROUP_WIDTH:(n + 1) * GROUP_WIDTH] = o[...] * g[...]
    y_ref[...] = jnp.dot(a_ref[...], w_ref[...], preferred_element_type=F32)
    pg = pg_ref[...]

    def body(r, carry):
        sl = pl.ds(pl.multiple_of(r * OUT_RC, OUT_RC), OUT_RC)
        y = y_ref[sl, :]
        yn = y * lax.rsqrt(jnp.mean(y * y, axis=-1, keepdims=True) + EPS) * pg
        out_ref[sl, :] = h_ref[sl, :] + gate * yn
        return carry

    lax.fori_loop(0, OUT_TM // OUT_RC, body, 0)


def _out_projection(l, outs, p, hh, mod, post_g, w_out, with_ctx):
    rows = N_TOK if with_ctx else N_LAT
    o_spec = pl.BlockSpec((OUT_TM, GROUP_WIDTH), lambda i: (i, 0))
    gate_spec = lambda n: pl.BlockSpec((OUT_TM, GROUP_WIDTH), lambda i: (i, COL_GATE // GROUP_WIDTH + n))
    return pl.pallas_call(
        _outproj_kernel,
        grid=(rows // OUT_TM,),
        in_specs=[
            o_spec, o_spec, o_spec, o_spec,
            gate_spec(0), gate_spec(1), gate_spec(2), gate_spec(3),
            pl.BlockSpec((OUT_TM, D_MODEL), lambda i: (i, 0)),
            pl.BlockSpec((None, 8, 3 * D_MODEL), lambda i: (l, 0, 0)),
            pl.BlockSpec((1, D_MODEL), lambda i: (0, 0)),
            pl.BlockSpec((None, MIX_WIDTH, D_MODEL), lambda i: (l, 0, 0)),
        ],
        out_specs=pl.BlockSpec((OUT_TM, D_MODEL), lambda i: (i, 0)),
        out_shape=jax.ShapeDtypeStruct((rows, D_MODEL), F32),
        scratch_shapes=[pltpu.VMEM((OUT_TM, MIX_WIDTH), BF16), pltpu.VMEM((OUT_TM, D_MODEL), F32)],
        compiler_params=_params(("arbitrary",), 56),
        name="out_projection",
    )(*outs, p, p, p, p, hh, mod, post_g.reshape(1, D_MODEL), w_out)


def kernel(x, c, ctx, c_ctx, w_mod, b_mod, pre_norm_g, w_in, diff_lambda_q1, diff_lambda_k1, diff_lambda_q2,
           diff_lambda_k2, diff_subln_g, win_sink, na_rpb, qk_q_norm_g, qk_k_norm_g, w_out, post_norm_g):
    assert x.shape == (BATCH, SEQ, D_MODEL) and ctx.shape == (BATCH, CTX_LEN, D_MODEL)
    cc = jnp.concatenate([c, c_ctx[None], jnp.zeros((8 - BATCH - 1, D_MODEL), F32)], axis=0)
    mod = _modulation(cc, w_mod, b_mod)
    tables = _rope_tables()
    na_bias = _neigh_bias(na_rpb)
    w_in_b = w_in.astype(BF16)
    w_out_b = w_out.astype(BF16)
    hh = jnp.concatenate([x.reshape(N_LAT, D_MODEL), ctx.reshape(N_CTX, D_MODEL)], axis=0)

    vec = lambda a, n: a.reshape(1, n)
    for l in range(DEPTH):
        with_ctx = l < DEPTH - 1
        lambda_init = 0.8 - 0.6 * math.exp(-0.3 * l)
        p = _in_projection(l, hh, mod, pre_norm_g[l], w_in_b, tables, qk_q_norm_g[l], qk_k_norm_g[l])

        lam_vecs = [(_const_spec((1, DIFF_DIM)), vec(a[l], DIFF_DIM))
                    for a in (diff_lambda_q1, diff_lambda_k1, diff_lambda_q2, diff_lambda_k2)]
        oa = _attention_call(
            functools.partial(_diff_kernel, lambda_init=lambda_init, with_ctx=with_ctx), "diff_attention", p,
            q_col=COL_A_Q, k_col=COL_A_K, v_col=COL_A_V, q_heads=1, n_kv=GROUP_HEADS, with_ctx=with_ctx,
            post=lam_vecs + [(_const_spec((1, HEAD_DIM)), vec(diff_subln_g[l], HEAD_DIM))])
        ob = _attention_call(
            functools.partial(_window_kernel, with_ctx=with_ctx), "window_attention", p,
            q_col=COL_B_Q, k_col=COL_B_K, v_col=COL_B_V, q_heads=GROUP_HEADS // KV_HEADS, n_kv=KV_HEADS,
            with_ctx=with_ctx, pre=[(pl.BlockSpec(memory_space=pltpu.SMEM), win_sink[l])])
        bias_spec = pl.BlockSpec(
            (None, None, None, TQ, NA_KEYS),
            lambda b, h, i, l=l: (l, h, jnp.where(i == 0, 0, jnp.where(i >= NQ - 1, 2, 1)), 0, 0))
        on = _attention_call(
            functools.partial(_neigh_kernel, with_ctx=with_ctx), "neighborhood_attention", p,
            q_col=COL_C_Q, k_col=COL_C_K, v_col=COL_C_V, q_heads=1, n_kv=GROUP_HEADS, with_ctx=with_ctx,
            post=[(bias_spec, na_bias)])
        od = _attention_call(
            functools.partial(_qknorm_kernel, with_ctx=with_ctx), "qknorm_attention", p,
            q_col=COL_D_Q, k_col=COL_D_K, v_col=COL_D_V, q_heads=GROUP_HEADS // KV_HEADS, n_kv=KV_HEADS,
            with_ctx=with_ctx)

        hh = _out_projection(l, (oa, ob, on, od), p, hh, mod, post_norm_g[l], w_out_b, with_ctx)
    return hh.reshape(BATCH, SEQ, D_MODEL)
```

```python
import functools
import math

import numpy as np
import jax
import jax.numpy as jnp
from jax import lax
from jax.experimental import pallas as pl
from jax.experimental.pallas import tpu as pltpu

D_MODEL = 2048
BATCH = 4
SEQ = 2048
DEPTH = 4
GRID_W = 64
CTX_LEN = 256
HEAD_DIM = 128
N_GROUPS = 4
GROUP_HEADS = 4
GROUP_WIDTH = GROUP_HEADS * HEAD_DIM
MIX_WIDTH = N_GROUPS * GROUP_WIDTH
KV_HEADS = 2
KV_WIDTH = KV_HEADS * HEAD_DIM
DIFF_DIM = HEAD_DIM // 2
WINDOW = 128
NA_WIN_H = 8
NA_WIN_W = 16
ROPE_THETA = 10000.0
EPS = 1e-6
NEG_INF = -1e30

IN_WIDTH = 3 * GROUP_WIDTH + (GROUP_WIDTH + 2 * KV_WIDTH) + 3 * GROUP_WIDTH + (GROUP_WIDTH + 2 * KV_WIDTH) + MIX_WIDTH
COL_A_Q, COL_A_K, COL_A_V = 0, 512, 1024
COL_B_Q, COL_B_K, COL_B_V = 1536, 2048, 2304
COL_C_Q, COL_C_K, COL_C_V = 2560, 3072, 3584
COL_D_Q, COL_D_K, COL_D_V = 4096, 4608, 4864
COL_GATE = 5120

N_LAT = BATCH * SEQ
N_CTX = BATCH * CTX_LEN
N_TOK = N_LAT + N_CTX
ROWS = SEQ // GRID_W

F32 = jnp.float32
BF16 = jnp.bfloat16

MIB = 1024 * 1024

TQ = 256
NQ = SEQ // TQ
IN_TM = 1024
IN_TN = 256
IN_RC = 128
IN_NORM_RC = 128
OUT_TM = 512
OUT_RC = 64
MOD_TN = 768
NA_QROWS = TQ // GRID_W
NA_KROWS = 12
NA_KEYS = NA_KROWS * GRID_W
NA_HEADS_PER_STEP = 4
DIFF_HEADS_PER_STEP = 2
WIN_KEYS = TQ + 2 * WINDOW

_NT = (((1,), (1,)), ((), ()))
_TN = (((0,), (0,)), ((), ()))


def _params(semantics, vmem_mib):
    return pltpu.CompilerParams(dimension_semantics=semantics, vmem_limit_bytes=vmem_mib * MIB)


def _silu(x):
    return x / (1.0 + jnp.exp(-x))


def _mod_kernel(cc_ref, w_ref, b_ref, o_ref):
    sc = _silu(cc_ref[...])
    o_ref[...] = jnp.dot(sc.astype(BF16), w_ref[...].astype(BF16), preferred_element_type=F32) + b_ref[...]


def _modulation(cc, w_mod, b_mod):
    n = 3 * D_MODEL
    return pl.pallas_call(
        _mod_kernel,
        grid=(DEPTH, n // MOD_TN),
        in_specs=[
            pl.BlockSpec((8, D_MODEL), lambda l, j: (0, 0)),
            pl.BlockSpec((None, D_MODEL, MOD_TN), lambda l, j: (l, 0, j)),
            pl.BlockSpec((None, 1, MOD_TN), lambda l, j: (l, 0, j)),
        ],
        out_specs=pl.BlockSpec((None, 8, MOD_TN), lambda l, j: (l, 0, j)),
        out_shape=jax.ShapeDtypeStruct((DEPTH, 8, n), F32),
        compiler_params=_params(("arbitrary", "arbitrary"), 40),
        name="modulation",
    )(cc, w_mod, b_mod.reshape(DEPTH, 1, n))


_K_PLAIN, _K_ROPE_SUB_Q, _K_ROPE_SUB, _K_ROPE_Q, _K_ROPE, _K_SCALE, _K_NORM_ROPE_Q, _K_NORM_ROPE, _K_SILU = range(9)
_SECTIONS = (
    (COL_A_Q, COL_A_K, _K_ROPE_SUB_Q), (COL_A_K, COL_A_V, _K_ROPE_SUB), (COL_A_V, COL_B_Q, _K_PLAIN),
    (COL_B_Q, COL_B_K, _K_ROPE_Q), (COL_B_K, COL_B_V, _K_ROPE), (COL_B_V, COL_C_Q, _K_PLAIN),
    (COL_C_Q, COL_C_K, _K_SCALE), (COL_C_K, COL_D_Q, _K_PLAIN),
    (COL_D_Q, COL_D_K, _K_NORM_ROPE_Q), (COL_D_K, COL_D_V, _K_NORM_ROPE), (COL_D_V, COL_GATE, _K_PLAIN),
    (COL_GATE, IN_WIDTH, _K_SILU),
)
LOG2E = math.log2(math.e)
_HEAD_SCALE = HEAD_DIM ** -0.5 * LOG2E
_SUB_SCALE = DIFF_DIM ** -0.5 * LOG2E


def _kind_pred(j, kind):
    pred = None
    for lo, hi, k in _SECTIONS:
        if k != kind:
            continue
        p = jnp.logical_and(j >= lo // IN_TN, j < hi // IN_TN)
        pred = p if pred is None else jnp.logical_or(pred, p)
    return pred


def _head_rms(x, g):
    return x * lax.rsqrt(jnp.mean(x * x, axis=-1, keepdims=True) + EPS) * g


def _inproj_kernel(x_ref, mod_ref, g_ref, w_ref, ch_ref, sh_ref, cs_ref, sa_ref, sb_ref, gq_ref, gk_ref,
                   o_ref, n_ref, acc_ref):
    i = pl.program_id(0)
    j = pl.program_id(1)
    n_chunks = IN_TM // IN_RC

    @pl.when(j == 0)
    def _():
        row = jnp.minimum(i // (SEQ // IN_TM), BATCH)
        shift = mod_ref[pl.ds(row, 1), 0:D_MODEL]
        scale1 = 1.0 + mod_ref[pl.ds(row, 1), D_MODEL:2 * D_MODEL]
        g = g_ref[...]

        def body(r, carry):
            sl = pl.ds(pl.multiple_of(r * IN_NORM_RC, IN_NORM_RC), IN_NORM_RC)
            inv = lax.rsqrt(jnp.mean(jnp.square(x_ref[sl, :]), axis=-1, keepdims=True) + EPS)
            n_ref[sl, :] = (x_ref[sl, :] * inv * g * scale1 + shift).astype(BF16)
            return carry

        lax.fori_loop(0, IN_TM // IN_NORM_RC, body, 0)

    acc_ref[...] = jnp.dot(n_ref[...], w_ref[...], preferred_element_type=F32)

    def epilogue(kind, fn, unroll=4):
        @pl.when(_kind_pred(j, kind))
        def _():
            def body(r, carry):
                sl = pl.ds(pl.multiple_of(r * IN_RC, IN_RC), IN_RC)
                for half in range(IN_TN // HEAD_DIM):
                    cols = slice(half * HEAD_DIM, (half + 1) * HEAD_DIM)
                    o_ref[sl, cols] = fn(acc_ref[sl, cols], sl).astype(BF16)
                return carry

            lax.fori_loop(0, n_chunks, body, 0, unroll=unroll)

    def rope_head(x, sl):
        return x * ch_ref[sl, :] + pltpu.roll(x, HEAD_DIM // 2, 1) * sh_ref[sl, :]

    def rope_sub(x, sl):
        return (x * cs_ref[sl, :] + pltpu.roll(x, HEAD_DIM - DIFF_DIM // 2, 1) * sa_ref[sl, :]
                + pltpu.roll(x, DIFF_DIM // 2, 1) * sb_ref[sl, :])

    epilogue(_K_PLAIN, lambda x, sl: x)
    epilogue(_K_ROPE_SUB_Q, lambda x, sl: rope_sub(x, sl) * _SUB_SCALE)
    epilogue(_K_ROPE_SUB, rope_sub)
    epilogue(_K_ROPE_Q, lambda x, sl: rope_head(x, sl) * _HEAD_SCALE)
    epilogue(_K_ROPE, rope_head)
    epilogue(_K_SCALE, lambda x, sl: x * _HEAD_SCALE)
    epilogue(_K_NORM_ROPE_Q, lambda x, sl: rope_head(_head_rms(x, gq_ref[...]), sl) * _HEAD_SCALE, unroll=2)
    epilogue(_K_NORM_ROPE, lambda x, sl: rope_head(_head_rms(x, gk_ref[...]), sl), unroll=2)
    epilogue(_K_SILU, lambda x, sl: _silu(x))


def _in_projection(l, hh, mod, pre_g, w_in, tables, gq, gk):
    lat_blocks = N_LAT // IN_TM
    per_seq = SEQ // IN_TM

    def tab_map(i, j):
        return (jnp.where(i < lat_blocks, i % per_seq, per_seq), 0)

    tab_spec = pl.BlockSpec((IN_TM, HEAD_DIM), tab_map)
    vec_spec = lambda n: pl.BlockSpec((1, n), lambda i, j: (0, 0))
    return pl.pallas_call(
        _inproj_kernel,
        grid=(N_TOK // IN_TM, IN_WIDTH // IN_TN),
        in_specs=[
            pl.BlockSpec((IN_TM, D_MODEL), lambda i, j: (i, 0)),
            pl.BlockSpec((None, 8, 3 * D_MODEL), lambda i, j: (l, 0, 0)),
            vec_spec(D_MODEL),
            pl.BlockSpec((None, D_MODEL, IN_TN), lambda i, j: (l, 0, j)),
            tab_spec, tab_spec, tab_spec, tab_spec, tab_spec,
            vec_spec(HEAD_DIM), vec_spec(HEAD_DIM),
        ],
        out_specs=pl.BlockSpec((IN_TM, IN_TN), lambda i, j: (i, j)),
        out_shape=jax.ShapeDtypeStruct((N_TOK, IN_WIDTH), BF16),
        scratch_shapes=[pltpu.VMEM((IN_TM, D_MODEL), BF16), pltpu.VMEM((IN_TM, IN_TN), F32)],
        compiler_params=_params(("arbitrary", "arbitrary"), 48),
        name="in_projection",
    )(hh, mod, pre_g.reshape(1, D_MODEL), w_in, *tables, gq.reshape(1, HEAD_DIM), gk.reshape(1, HEAD_DIM))


def _rope_tables():
    t = jnp.arange(SEQ)
    row = (t // GRID_W).astype(F32)
    col = (t % GRID_W).astype(F32)

    def cos_sin(dim):
        n_freq = dim // 4
        inv = ROPE_THETA ** (-jnp.arange(n_freq, dtype=F32) / n_freq)
        ang = jnp.concatenate([row[:, None] * inv, col[:, None] * inv], axis=-1)
        return jnp.cos(ang), jnp.sin(ang)

    ch, sh = cos_sin(HEAD_DIM)
    cs, ss = cos_sin(DIFF_DIM)
    zs = jnp.zeros_like(ss)
    tabs = [
        jnp.concatenate([ch, ch], axis=-1),
        jnp.concatenate([-sh, sh], axis=-1),
        jnp.concatenate([cs, cs, cs, cs], axis=-1),
        jnp.concatenate([-ss, zs, -ss, zs], axis=-1),
        jnp.concatenate([zs, ss, zs, ss], axis=-1),
    ]
    ident = [jnp.ones, jnp.zeros, jnp.ones, jnp.zeros, jnp.zeros]
    return [jnp.concatenate([tb, f((IN_TM, HEAD_DIM), F32)], axis=0) for tb, f in zip(tabs, ident)]


def _softmax_pv_t(q, segs, extra=None):
    scores = []
    m = extra
    for k, v, fix in segs:
        s = lax.dot_general(k, q, _NT, preferred_element_type=F32)
        if fix is not None:
            s = fix(s)
        scores.append(s)
        sm = jnp.max(s, axis=0, keepdims=True)
        m = sm if m is None else jnp.maximum(m, sm)
    l = None if extra is None else jnp.exp2(extra - m)
    o = None
    for s, (k, v, fix) in zip(scores, segs):
        e = jnp.exp2(s - m)
        ls = jnp.sum(e, axis=0, keepdims=True)
        l = ls if l is None else l + ls
        pv = lax.dot_general(v, e.astype(BF16), _TN, preferred_element_type=F32)
        o = pv if o is None else o + pv
    return o / l


def _stacked_heads(q_ref, g):
    return jnp.concatenate([q_ref[:, h * HEAD_DIM:(h + 1) * HEAD_DIM] for h in range(g)], axis=0)


def _store_heads(o_ref, ot, g):
    for h in range(g):
        o_ref[:, h * HEAD_DIM:(h + 1) * HEAD_DIM] = ot[:, h * TQ:(h + 1) * TQ].T.astype(BF16)


def _is_ctx_step():
    return pl.program_id(2) == NQ


def _q_row_map(b, h, i):
    return jnp.where(i < NQ, b * NQ + i, BATCH * NQ + b)


def _diff_kernel(q_ref, kl_ref, kc_ref, vl_ref, vc_ref, lq1_ref, lk1_ref, lq2_ref, lk2_ref, g_ref, o_ref, *,
                 lambda_init, with_ctx):
    lam = (jnp.exp(jnp.sum(lq1_ref[...] * lk1_ref[...], axis=-1, keepdims=True))
           - jnp.exp(jnp.sum(lq2_ref[...] * lk2_ref[...], axis=-1, keepdims=True)) + lambda_init)

    def run(h, lat):
        head = slice(h * HEAD_DIM, (h + 1) * HEAD_DIM)
        q = q_ref[:, head]
        lane = lax.broadcasted_iota(jnp.int32, q.shape, 1)
        zero = jnp.zeros_like(q)
        qq = jnp.concatenate([jnp.where(lane < DIFF_DIM, q, zero), jnp.where(lane >= DIFF_DIM, q, zero)], axis=0)
        segs = [(kc_ref[:, head], vc_ref[:, head], None)]
        if lat:
            segs.insert(0, (kl_ref[:, head], vl_ref[:, head], None))
        ot = _softmax_pv_t(qq, segs)
        ot = ot[:, :TQ] - lam * ot[:, TQ:]
        ot = ot * lax.rsqrt(jnp.mean(ot * ot, axis=0, keepdims=True) + EPS) * g_ref[...]
        o_ref[:, head] = (ot * (1.0 - lambda_init)).T.astype(BF16)

    if with_ctx:
        @pl.when(_is_ctx_step())
        def _():
            for h in range(DIFF_HEADS_PER_STEP):
                run(h, False)

    @pl.when(jnp.logical_not(_is_ctx_step()))
    def _():
        for h in range(DIFF_HEADS_PER_STEP):
            run(h, True)


def _window_kernel(sink_ref, q_ref, kl_ref, kc_ref, vl_ref, vc_ref, o_ref, *, with_ctx):
    i = pl.program_id(2)
    g = GROUP_HEADS // KV_HEADS
    cols = lax.broadcasted_iota(jnp.int32, (1, g * TQ), 1)

    def run(kvh, lat):
        head = slice(kvh * HEAD_DIM, (kvh + 1) * HEAD_DIM)
        sink = jnp.where(cols < TQ, sink_ref[kvh * g], sink_ref[kvh * g + 1]).astype(F32) * LOG2E
        q = _stacked_heads(q_ref.at[:, kvh * g * HEAD_DIM:(kvh + 1) * g * HEAD_DIM], g)
        segs = [(kc_ref[:, head], vc_ref[:, head], None)]
        if lat:
            q0 = i * TQ
            ws = pl.multiple_of(jnp.clip(q0 - WINDOW, 0, SEQ - WIN_KEYS), WINDOW)
            kpos = ws + lax.broadcasted_iota(jnp.int32, (WIN_KEYS, g * TQ), 0)
            qpos = q0 + (lax.broadcasted_iota(jnp.int32, (WIN_KEYS, g * TQ), 1) & (TQ - 1))
            valid = jnp.abs(kpos - qpos) <= WINDOW
            band = lambda s: jnp.where(valid, s, NEG_INF)
            segs.insert(0, (kl_ref[pl.ds(ws, WIN_KEYS), head], vl_ref[pl.ds(ws, WIN_KEYS), head], band))
        _store_heads(o_ref.at[:, kvh * g * HEAD_DIM:(kvh + 1) * g * HEAD_DIM], _softmax_pv_t(q, segs, extra=sink), g)

    if with_ctx:
        @pl.when(_is_ctx_step())
        def _():
            for kvh in range(KV_HEADS):
                run(kvh, False)

    @pl.when(jnp.logical_not(_is_ctx_step()))
    def _():
        for kvh in range(KV_HEADS):
            run(kvh, True)


def _neigh_kernel(q_ref, kl_ref, kc_ref, vl_ref, vc_ref, bias_ref, o_ref, *, with_ctx):
    i = pl.program_id(2)

    def run(h, lat):
        head = slice(h * HEAD_DIM, (h + 1) * HEAD_DIM)
        segs = [(kc_ref[:, head], vc_ref[:, head], None)]
        if lat:
            wr0 = jnp.clip(i * NA_QROWS - NA_WIN_H // 2, 0, ROWS - NA_KROWS)
            ws = pl.multiple_of(wr0 * GRID_W, GRID_W)
            add_bias = lambda s: s + bias_ref[h]
            segs.insert(0, (kl_ref[pl.ds(ws, NA_KEYS), head], vl_ref[pl.ds(ws, NA_KEYS), head], add_bias))
        o_ref[:, head] = _softmax_pv_t(q_ref[:, head], segs).T.astype(BF16)

    if with_ctx:
        @pl.when(_is_ctx_step())
        def _():
            for h in range(NA_HEADS_PER_STEP):
                run(h, False)

    @pl.when(jnp.logical_not(_is_ctx_step()))
    def _():
        for h in range(NA_HEADS_PER_STEP):
            run(h, True)


def _qknorm_kernel(q_ref, kl_ref, kc_ref, vl_ref, vc_ref, o_ref, *, with_ctx):
    g = GROUP_HEADS // KV_HEADS

    def run(kvh, lat):
        head = slice(kvh * HEAD_DIM, (kvh + 1) * HEAD_DIM)
        group = slice(kvh * g * HEAD_DIM, (kvh + 1) * g * HEAD_DIM)
        segs = [(kc_ref[:, head], vc_ref[:, head], None)]
        if lat:
            segs.insert(0, (kl_ref[:, head], vl_ref[:, head], None))
        _store_heads(o_ref.at[:, group], _softmax_pv_t(_stacked_heads(q_ref.at[:, group], g), segs), g)

    if with_ctx:
        @pl.when(_is_ctx_step())
        def _():
            for kvh in range(KV_HEADS):
                run(kvh, False)

    @pl.when(jnp.logical_not(_is_ctx_step()))
    def _():
        for kvh in range(KV_HEADS):
            run(kvh, True)


def _attention_call(kernel, name, p, *, q_col, k_col, v_col, q_heads, kv_heads, n_groups, with_ctx, pre=(), post=(),
                    vmem_mib=48):
    qw = q_heads * HEAD_DIM
    kvw = kv_heads * HEAD_DIM
    n_kv = n_groups
    ctx_blk0 = N_LAT // CTX_LEN
    steps = NQ + 1 if with_ctx else NQ
    q_spec = pl.BlockSpec((TQ, qw), lambda b, h, i: (_q_row_map(b, h, i), q_col // qw + h))
    kv_lat = lambda col: pl.BlockSpec((SEQ, kvw), lambda b, h, i: (b, col // kvw + h))
    kv_ctx = lambda col: pl.BlockSpec((CTX_LEN, kvw), lambda b, h, i: (ctx_blk0 + b, col // kvw + h))
    pre_specs, pre_args = zip(*pre) if pre else ((), ())
    post_specs, post_args = zip(*post) if post else ((), ())
    rows = N_TOK if with_ctx else N_LAT
    return pl.pallas_call(
        kernel,
        grid=(BATCH, n_kv, steps),
        in_specs=[*pre_specs, q_spec, kv_lat(k_col), kv_ctx(k_col), kv_lat(v_col), kv_ctx(v_col), *post_specs],
        out_specs=pl.BlockSpec((TQ, qw), lambda b, h, i: (_q_row_map(b, h, i), h)),
        out_shape=jax.ShapeDtypeStruct((rows, GROUP_WIDTH), BF16),
        compiler_params=_params(("arbitrary", "arbitrary", "arbitrary"), vmem_mib),
        name=name,
    )(*pre_args, p, p, p, p, p, *post_args)


def _const_spec(shape):
    return pl.BlockSpec(shape, lambda b, h, i: (0,) * len(shape))


def _neigh_bias(rpb):
    n_rel_r, n_rel_c = 2 * NA_WIN_H - 1, 2 * NA_WIN_W - 1
    ck = np.arange(GRID_W)[:, None]
    cq = np.arange(GRID_W)[None, :]
    cstart = np.clip(cq - NA_WIN_W // 2, 0, GRID_W - NA_WIN_W)
    col_ok = (ck >= cstart) & (ck < cstart + NA_WIN_W)
    rel_c = np.clip(ck - cq + NA_WIN_W - 1, 0, n_rel_c - 1)
    pick_c = (rel_c[None] == np.arange(n_rel_c)[:, None, None]).astype(np.float32)
    strip = jnp.einsum("dhrc,ckq->dhrkq", rpb.astype(F32), pick_c, precision=lax.Precision.HIGHEST) * LOG2E
    strip = jnp.where(col_ok, strip, NEG_INF).reshape(DEPTH, GROUP_HEADS, n_rel_r * GRID_W, GRID_W)

    def neg(n_rows):
        return jnp.full((DEPTH, GROUP_HEADS, n_rows * GRID_W, GRID_W), NEG_INF, F32)

    blocks = []
    for blk in (0, 1, NQ - 1):
        r0 = blk * NA_QROWS
        wr0 = int(np.clip(r0 - NA_WIN_H // 2, 0, ROWS - NA_KROWS))
        q_rows = []
        for rq in range(r0, r0 + NA_QROWS):
            rstart = int(np.clip(rq - NA_WIN_H // 2, 0, ROWS - NA_WIN_H))
            before = rstart - wr0
            after = NA_KROWS - NA_WIN_H - before
            rel0 = rstart - rq + NA_WIN_H - 1
            pieces = [neg(before)] if before else []
            pieces.append(strip[:, :, rel0 * GRID_W:(rel0 + NA_WIN_H) * GRID_W, :])
            if after:
                pieces.append(neg(after))
            q_rows.append(jnp.concatenate(pieces, axis=2))
        blocks.append(jnp.concatenate(q_rows, axis=3))
    return jnp.stack(blocks, axis=2)


def _outproj_kernel(oa_ref, ob_ref, oc_ref, od_ref, ga_ref, gb_ref, gc_ref, gd_ref, h_ref, mod_ref, pg_ref, w_ref,
                    out_ref, a_ref, y_ref):
    i = pl.program_id(0)
    row = jnp.minimum(i // (SEQ // OUT_TM), BATCH)
    gate = mod_ref[pl.ds(row, 1), 2 * D_MODEL:3 * D_MODEL]
    for n, (o, g) in enumerate(((oa_ref, ga_ref), (ob_ref, gb_ref), (oc_ref, gc_ref), (od_ref, gd_ref))):
        a_ref[:, n * GROUP_WIDTH:(n + 1) * GROUP_WIDTH] = o[...] * g[...]
    y_ref[...] = jnp.dot(a_ref[...], w_ref[...], preferred_element_type=F32)
    pg = pg_ref[...]

    def body(r, carry):
        sl = pl.ds(pl.multiple_of(r * OUT_RC, OUT_RC), OUT_RC)
        y = y_ref[sl, :]
        yn = y * lax.rsqrt(jnp.mean(y * y, axis=-1, keepdims=True) + EPS) * pg
        out_ref[sl, :] = h_ref[sl, :] + gate * yn
        return carry

    lax.fori_loop(0, OUT_TM // OUT_RC, body, 0)


def _out_projection(l, outs, p, hh, mod, post_g, w_out, with_ctx):
    rows = N_TOK if with_ctx else N_LAT
    o_spec = pl.BlockSpec((OUT_TM, GROUP_WIDTH), lambda i: (i, 0))
    gate_spec = lambda n: pl.BlockSpec((OUT_TM, GROUP_WIDTH), lambda i: (i, COL_GATE // GROUP_WIDTH + n))
    return pl.pallas_call(
        _outproj_kernel,
        grid=(rows // OUT_TM,),
        in_specs=[
            o_spec, o_spec, o_spec, o_spec,
            gate_spec(0), gate_spec(1), gate_spec(2), gate_spec(3),
            pl.BlockSpec((OUT_TM, D_MODEL), lambda i: (i, 0)),
            pl.BlockSpec((None, 8, 3 * D_MODEL), lambda i: (l, 0, 0)),
            pl.BlockSpec((1, D_MODEL), lambda i: (0, 0)),
            pl.BlockSpec((None, MIX_WIDTH, D_MODEL), lambda i: (l, 0, 0)),
        ],
        out_specs=pl.BlockSpec((OUT_TM, D_MODEL), lambda i: (i, 0)),
        out_shape=jax.ShapeDtypeStruct((rows, D_MODEL), F32),
        scratch_shapes=[pltpu.VMEM((OUT_TM, MIX_WIDTH), BF16), pltpu.VMEM((OUT_TM, D_MODEL), F32)],
        compiler_params=_params(("arbitrary",), 56),
        name="out_projection",
    )(*outs, p, p, p, p, hh, mod, post_g.reshape(1, D_MODEL), w_out)


def kernel(x, c, ctx, c_ctx, w_mod, b_mod, pre_norm_g, w_in, diff_lambda_q1, diff_lambda_k1, diff_lambda_q2,
           diff_lambda_k2, diff_subln_g, win_sink, na_rpb, qk_q_norm_g, qk_k_norm_g, w_out, post_norm_g):
    assert x.shape == (BATCH, SEQ, D_MODEL) and ctx.shape == (BATCH, CTX_LEN, D_MODEL)
    cc = jnp.concatenate([c, c_ctx[None], jnp.zeros((8 - BATCH - 1, D_MODEL), F32)], axis=0)
    mod = _modulation(cc, w_mod, b_mod)
    tables = _rope_tables()
    na_bias = _neigh_bias(na_rpb)
    w_in_b = w_in.astype(BF16)
    w_out_b = w_out.astype(BF16)
    hh = jnp.concatenate([x.reshape(N_LAT, D_MODEL), ctx.reshape(N_CTX, D_MODEL)], axis=0)

    vec = lambda a, n: a.reshape(1, n)
    for l in range(DEPTH):
        with_ctx = l < DEPTH - 1
        lambda_init = 0.8 - 0.6 * math.exp(-0.3 * l)
        p = _in_projection(l, hh, mod, pre_norm_g[l], w_in_b, tables, qk_q_norm_g[l], qk_k_norm_g[l])

        lam_vecs = [(_const_spec((1, DIFF_DIM)), vec(a[l], DIFF_DIM))
                    for a in (diff_lambda_q1, diff_lambda_k1, diff_lambda_q2, diff_lambda_k2)]
        oa = _attention_call(
            functools.partial(_diff_kernel, lambda_init=lambda_init, with_ctx=with_ctx), "diff_attention", p,
            q_col=COL_A_Q, k_col=COL_A_K, v_col=COL_A_V, q_heads=DIFF_HEADS_PER_STEP,
            kv_heads=DIFF_HEADS_PER_STEP, n_groups=GROUP_HEADS // DIFF_HEADS_PER_STEP,
            with_ctx=with_ctx,
            post=lam_vecs + [(_const_spec((HEAD_DIM, 1)), diff_subln_g[l].reshape(HEAD_DIM, 1))])
        ob = _attention_call(
            functools.partial(_window_kernel, with_ctx=with_ctx), "window_attention", p,
            q_col=COL_B_Q, k_col=COL_B_K, v_col=COL_B_V, q_heads=GROUP_HEADS, kv_heads=KV_HEADS, n_groups=1,
            with_ctx=with_ctx, pre=[(pl.BlockSpec(memory_space=pltpu.SMEM), win_sink[l])])
        bias_spec = pl.BlockSpec(
            (None, NA_HEADS_PER_STEP, None, NA_KEYS, TQ),
            lambda b, h, i, l=l: (l, h, jnp.where(i == 0, 0, jnp.where(i >= NQ - 1, 2, 1)), 0, 0))
        on = _attention_call(
            functools.partial(_neigh_kernel, with_ctx=with_ctx), "neighborhood_attention", p,
            q_col=COL_C_Q, k_col=COL_C_K, v_col=COL_C_V, q_heads=NA_HEADS_PER_STEP, kv_heads=NA_HEADS_PER_STEP,
            n_groups=GROUP_HEADS // NA_HEADS_PER_STEP, with_ctx=with_ctx, post=[(bias_spec, na_bias)])
        od = _attention_call(
            functools.partial(_qknorm_kernel, with_ctx=with_ctx), "qknorm_attention", p,
            q_col=COL_D_Q, k_col=COL_D_K, v_col=COL_D_V, q_heads=GROUP_HEADS, kv_heads=KV_HEADS, n_groups=1,
            with_ctx=with_ctx)

        hh = _out_projection(l, (oa, ob, on, od), p, hh, mod, post_norm_g[l], w_out_b, with_ctx)
    return hh.reshape(BATCH, SEQ, D_MODEL)
```

```python
import functools
import math

import numpy as np
import jax
import jax.numpy as jnp
from jax import lax
from jax.experimental import pallas as pl
from jax.experimental.pallas import tpu as pltpu

D_MODEL = 2048
BATCH = 4
SEQ = 2048
DEPTH = 4
GRID_W = 64
CTX_LEN = 256
HEAD_DIM = 128
N_GROUPS = 4
GROUP_HEADS = 4
GROUP_WIDTH = GROUP_HEADS * HEAD_DIM
MIX_WIDTH = N_GROUPS * GROUP_WIDTH
KV_HEADS = 2
KV_WIDTH = KV_HEADS * HEAD_DIM
DIFF_DIM = HEAD_DIM // 2
WINDOW = 128
NA_WIN_H = 8
NA_WIN_W = 16
ROPE_THETA = 10000.0
EPS = 1e-6
NEG_INF = -1e30

IN_WIDTH = 3 * GROUP_WIDTH + (GROUP_WIDTH + 2 * KV_WIDTH) + 3 * GROUP_WIDTH + (GROUP_WIDTH + 2 * KV_WIDTH) + MIX_WIDTH
COL_A_Q, COL_A_K, COL_A_V = 0, 512, 1024
COL_B_Q, COL_B_K, COL_B_V = 1536, 2048, 2304
COL_C_Q, COL_C_K, COL_C_V = 2560, 3072, 3584
COL_D_Q, COL_D_K, COL_D_V = 4096, 4608, 4864
COL_GATE = 5120

N_LAT = BATCH * SEQ
N_CTX = BATCH * CTX_LEN
N_TOK = N_LAT + N_CTX
ROWS = SEQ // GRID_W

F32 = jnp.float32
BF16 = jnp.bfloat16

MIB = 1024 * 1024

TQ = 256
NQ = SEQ // TQ
IN_TM = 1024
IN_TN = 512
IN_SLABS = 4
OUT_TM = 512
OUT_SLABS = 2
MOD_TN = 768
NA_QROWS = TQ // GRID_W
NA_KROWS = 12
NA_KEYS = NA_KROWS * GRID_W
NA_HEADS_PER_STEP = 4
DIFF_HEADS_PER_STEP = 2
WIN_KEYS = TQ + 2 * WINDOW

_NT = (((1,), (1,)), ((), ()))
_TN = (((0,), (0,)), ((), ()))


def _params(semantics, vmem_mib):
    return pltpu.CompilerParams(dimension_semantics=semantics, vmem_limit_bytes=vmem_mib * MIB)


def _silu(x):
    return x / (1.0 + jnp.exp(-x))


def _mod_kernel(cc_ref, w_ref, b_ref, o_ref):
    sc = _silu(cc_ref[...])
    o_ref[...] = jnp.dot(sc.astype(BF16), w_ref[...].astype(BF16), preferred_element_type=F32) + b_ref[...]


def _modulation(cc, w_mod, b_mod):
    n = 3 * D_MODEL
    return pl.pallas_call(
        _mod_kernel,
        grid=(DEPTH, n // MOD_TN),
        in_specs=[
            pl.BlockSpec((8, D_MODEL), lambda l, j: (0, 0)),
            pl.BlockSpec((None, D_MODEL, MOD_TN), lambda l, j: (l, 0, j)),
            pl.BlockSpec((None, 1, MOD_TN), lambda l, j: (l, 0, j)),
        ],
        out_specs=pl.BlockSpec((None, 8, MOD_TN), lambda l, j: (l, 0, j)),
        out_shape=jax.ShapeDtypeStruct((DEPTH, 8, n), F32),
        compiler_params=_params(("arbitrary", "arbitrary"), 40),
        name="modulation",
    )(cc, w_mod, b_mod.reshape(DEPTH, 1, n))


_K_PLAIN, _K_ROPE_SUB_Q, _K_ROPE_SUB, _K_ROPE_Q, _K_ROPE, _K_SCALE, _K_NORM_ROPE_Q, _K_NORM_ROPE, _K_SILU = range(9)
_SECTIONS = (
    (COL_A_Q, COL_A_K, _K_ROPE_SUB_Q), (COL_A_K, COL_A_V, _K_ROPE_SUB), (COL_A_V, COL_B_Q, _K_PLAIN),
    (COL_B_Q, COL_B_K, _K_ROPE_Q), (COL_B_K, COL_B_V, _K_ROPE), (COL_B_V, COL_C_Q, _K_PLAIN),
    (COL_C_Q, COL_C_K, _K_SCALE), (COL_C_K, COL_D_Q, _K_PLAIN),
    (COL_D_Q, COL_D_K, _K_NORM_ROPE_Q), (COL_D_K, COL_D_V, _K_NORM_ROPE), (COL_D_V, COL_GATE, _K_PLAIN),
    (COL_GATE, IN_WIDTH, _K_SILU),
)
LOG2E = math.log2(math.e)
_HEAD_SCALE = HEAD_DIM ** -0.5 * LOG2E
_SUB_SCALE = DIFF_DIM ** -0.5 * LOG2E


_UNIT_KINDS = [kind for lo, hi, kind in _SECTIONS for _ in range((hi - lo) // HEAD_DIM)]
_TILE_UNITS = IN_TN // HEAD_DIM
_TILE_PATTERNS = [tuple(_UNIT_KINDS[t * _TILE_UNITS:(t + 1) * _TILE_UNITS]) for t in range(IN_WIDTH // IN_TN)]


def _tile_pred(j, pattern):
    pred = None
    for t, pat in enumerate(_TILE_PATTERNS):
        if pat == pattern:
            pred = (j == t) if pred is None else jnp.logical_or(pred, j == t)
    return pred


def _head_rms(x, g):
    return x * lax.rsqrt(jnp.mean(x * x, axis=-1, keepdims=True) + EPS) * g


def _inproj_kernel(xl_ref, xc_ref, mod_ref, g_ref, w_ref, ch_ref, sh_ref, cs_ref, sa_ref, sb_ref, gq_ref, gk_ref,
                   o_ref, n_ref):
    i = pl.program_id(0)
    j = pl.program_id(1)
    lat_blocks = N_LAT // IN_TM

    def normed(x_ref, sl):
        row = jnp.minimum(i // (SEQ // IN_TM), BATCH)
        shift = mod_ref[pl.ds(row, 1), 0:D_MODEL]
        scale1 = 1.0 + mod_ref[pl.ds(row, 1), D_MODEL:2 * D_MODEL]
        x = x_ref[sl, :]
        inv = lax.rsqrt(jnp.mean(x * x, axis=-1, keepdims=True) + EPS)
        n = (x * inv * g_ref[...] * scale1 + shift).astype(BF16)
        n_ref[sl, :] = n
        return n

    def rope_head(x, sl):
        return x * ch_ref[sl, :] + pltpu.roll(x, HEAD_DIM // 2, 1) * sh_ref[sl, :]

    def rope_sub(x, sl):
        return (x * cs_ref[sl, :] + pltpu.roll(x, HEAD_DIM - DIFF_DIM // 2, 1) * sa_ref[sl, :]
                + pltpu.roll(x, DIFF_DIM // 2, 1) * sb_ref[sl, :])

    unit_fn = {
        _K_PLAIN: lambda x, sl: x,
        _K_ROPE_SUB_Q: lambda x, sl: rope_sub(x, sl) * _SUB_SCALE,
        _K_ROPE_SUB: rope_sub,
        _K_ROPE_Q: lambda x, sl: rope_head(x, sl) * _HEAD_SCALE,
        _K_ROPE: rope_head,
        _K_SCALE: lambda x, sl: x * _HEAD_SCALE,
        _K_NORM_ROPE_Q: lambda x, sl: rope_head(_head_rms(x, gq_ref[...]), sl) * _HEAD_SCALE,
        _K_NORM_ROPE: lambda x, sl: rope_head(_head_rms(x, gk_ref[...]), sl),
        _K_SILU: lambda x, sl: _silu(x),
    }

    assert _TILE_PATTERNS.count(_TILE_PATTERNS[0]) == 1, "the pre-norm rides on a pattern only tile 0 has"
    slab = IN_TM // IN_SLABS
    def tile(pattern, x_ref):
        for s in range(IN_SLABS):
            sl = slice(s * slab, (s + 1) * slab)
            lhs = n_ref[sl, :] if x_ref is None else normed(x_ref, sl)
            acc = jnp.dot(lhs, w_ref[...], preferred_element_type=F32)
            for u, kind in enumerate(pattern):
                cols = slice(u * HEAD_DIM, (u + 1) * HEAD_DIM)
                o_ref[sl, cols] = unit_fn[kind](acc[:, cols], sl).astype(BF16)

    for pattern in dict.fromkeys(_TILE_PATTERNS):
        pred = _tile_pred(j, pattern)
        if pattern == _TILE_PATTERNS[0]:
            pl.when(jnp.logical_and(pred, i < lat_blocks))(functools.partial(tile, pattern, xl_ref))
            pl.when(jnp.logical_and(pred, i >= lat_blocks))(functools.partial(tile, pattern, xc_ref))
        else:
            pl.when(pred)(functools.partial(tile, pattern, None))


def _in_projection(l, h_lat, h_ctx, mod, pre_g, w_in, tables, gq, gk):
    assert N_CTX == IN_TM, "the context tokens are exactly the last row block"
    lat_blocks = N_LAT // IN_TM
    per_seq = SEQ // IN_TM

    def tab_map(i, j):
        return (jnp.where(i < lat_blocks, i % per_seq, per_seq), 0)

    tab_spec = pl.BlockSpec((IN_TM, HEAD_DIM), tab_map)
    vec_spec = lambda n: pl.BlockSpec((1, n), lambda i, j: (0, 0))
    return pl.pallas_call(
        _inproj_kernel,
        grid=(N_TOK // IN_TM, IN_WIDTH // IN_TN),
        in_specs=[
            pl.BlockSpec((IN_TM, D_MODEL), lambda i, j: (jnp.minimum(i, lat_blocks - 1), 0)),
            pl.BlockSpec((IN_TM, D_MODEL), lambda i, j: (0, 0), pipeline_mode=pl.Buffered(1)),
            pl.BlockSpec((None, 8, 3 * D_MODEL), lambda i, j: (l, 0, 0)),
            vec_spec(D_MODEL),
            pl.BlockSpec((None, D_MODEL, IN_TN), lambda i, j: (l, 0, j)),
            tab_spec, tab_spec, tab_spec, tab_spec, tab_spec,
            vec_spec(HEAD_DIM), vec_spec(HEAD_DIM),
        ],
        out_specs=pl.BlockSpec((IN_TM, IN_TN), lambda i, j: (i, j)),
        out_shape=jax.ShapeDtypeStruct((N_TOK, IN_WIDTH), BF16),
        scratch_shapes=[pltpu.VMEM((IN_TM, D_MODEL), BF16)],
        compiler_params=_params(("arbitrary", "arbitrary"), 56),
        name="in_projection",
    )(h_lat, h_ctx, mod, pre_g.reshape(1, D_MODEL), w_in, *tables, gq.reshape(1, HEAD_DIM), gk.reshape(1, HEAD_DIM))


def _rope_tables():
    t = jnp.arange(SEQ)
    row = (t // GRID_W).astype(F32)
    col = (t % GRID_W).astype(F32)

    def cos_sin(dim):
        n_freq = dim // 4
        inv = ROPE_THETA ** (-jnp.arange(n_freq, dtype=F32) / n_freq)
        ang = jnp.concatenate([row[:, None] * inv, col[:, None] * inv], axis=-1)
        return jnp.cos(ang), jnp.sin(ang)

    ch, sh = cos_sin(HEAD_DIM)
    cs, ss = cos_sin(DIFF_DIM)
    zs = jnp.zeros_like(ss)
    tabs = [
        jnp.concatenate([ch, ch], axis=-1),
        jnp.concatenate([-sh, sh], axis=-1),
        jnp.concatenate([cs, cs, cs, cs], axis=-1),
        jnp.concatenate([-ss, zs, -ss, zs], axis=-1),
        jnp.concatenate([zs, ss, zs, ss], axis=-1),
    ]
    ident = [jnp.ones, jnp.zeros, jnp.ones, jnp.zeros, jnp.zeros]
    return [jnp.concatenate([tb, f((IN_TM, HEAD_DIM), F32)], axis=0) for tb, f in zip(tabs, ident)]


def _softmax_pv_t(q, segs, extra=None):
    scores = []
    m = extra
    for k, v, fix in segs:
        s = lax.dot_general(k, q, _NT, preferred_element_type=F32)
        if fix is not None:
            s = fix(s)
        scores.append(s)
        sm = jnp.max(s, axis=0, keepdims=True)
        m = sm if m is None else jnp.maximum(m, sm)
    l = None if extra is None else jnp.exp2(extra - m)
    o = None
    for s, (k, v, fix) in zip(scores, segs):
        e = jnp.exp2(s - m)
        ls = jnp.sum(e, axis=0, keepdims=True)
        l = ls if l is None else l + ls
        pv = lax.dot_general(v, e.astype(BF16), _TN, preferred_element_type=F32)
        o = pv if o is None else o + pv
    return o / l


def _stacked_heads(q_ref, g):
    return jnp.concatenate([q_ref[:, h * HEAD_DIM:(h + 1) * HEAD_DIM] for h in range(g)], axis=0)


def _store_heads(o_ref, ot, g):
    for h in range(g):
        o_ref[:, h * HEAD_DIM:(h + 1) * HEAD_DIM] = ot[:, h * TQ:(h + 1) * TQ].T.astype(BF16)


def _is_ctx_step():
    return pl.program_id(2) == NQ


def _q_row_map(b, h, i):
    return jnp.where(i < NQ, b * NQ + i, BATCH * NQ + b)


def _diff_kernel(q_ref, kl_ref, kc_ref, vl_ref, vc_ref, lq1_ref, lk1_ref, lq2_ref, lk2_ref, g_ref, o_ref, *,
                 lambda_init, with_ctx):
    lam = (jnp.exp(jnp.sum(lq1_ref[...] * lk1_ref[...], axis=-1, keepdims=True))
           - jnp.exp(jnp.sum(lq2_ref[...] * lk2_ref[...], axis=-1, keepdims=True)) + lambda_init)

    def run(h, lat):
        head = slice(h * HEAD_DIM, (h + 1) * HEAD_DIM)
        q = q_ref[:, head]
        lane = lax.broadcasted_iota(jnp.int32, q.shape, 1)
        zero = jnp.zeros_like(q)
        qq = jnp.concatenate([jnp.where(lane < DIFF_DIM, q, zero), jnp.where(lane >= DIFF_DIM, q, zero)], axis=0)
        segs = [(kc_ref[:, head], vc_ref[:, head], None)]
        if lat:
            segs.insert(0, (kl_ref[:, head], vl_ref[:, head], None))
        ot = _softmax_pv_t(qq, segs)
        ot = ot[:, :TQ] - lam * ot[:, TQ:]
        ot = ot * lax.rsqrt(jnp.mean(ot * ot, axis=0, keepdims=True) + EPS) * g_ref[...]
        o_ref[:, head] = (ot * (1.0 - lambda_init)).T.astype(BF16)

    if with_ctx:
        @pl.when(_is_ctx_step())
        def _():
            for h in range(DIFF_HEADS_PER_STEP):
                run(h, False)

    @pl.when(jnp.logical_not(_is_ctx_step()))
    def _():
        for h in range(DIFF_HEADS_PER_STEP):
            run(h, True)


def _window_kernel(sink_ref, q_ref, kl_ref, kc_ref, vl_ref, vc_ref, o_ref, *, with_ctx):
    i = pl.program_id(2)
    g = GROUP_HEADS // KV_HEADS
    cols = lax.broadcasted_iota(jnp.int32, (1, g * TQ), 1)

    def run(kvh, lat):
        head = slice(kvh * HEAD_DIM, (kvh + 1) * HEAD_DIM)
        sink = jnp.where(cols < TQ, sink_ref[kvh * g], sink_ref[kvh * g + 1]).astype(F32) * LOG2E
        q = _stacked_heads(q_ref.at[:, kvh * g * HEAD_DIM:(kvh + 1) * g * HEAD_DIM], g)
        segs = [(kc_ref[:, head], vc_ref[:, head], None)]
        if lat:
            q0 = i * TQ
            ws = pl.multiple_of(jnp.clip(q0 - WINDOW, 0, SEQ - WIN_KEYS), WINDOW)
            kpos = ws + lax.broadcasted_iota(jnp.int32, (WIN_KEYS, g * TQ), 0)
            qpos = q0 + (lax.broadcasted_iota(jnp.int32, (WIN_KEYS, g * TQ), 1) & (TQ - 1))
            valid = jnp.abs(kpos - qpos) <= WINDOW
            band = lambda s: jnp.where(valid, s, NEG_INF)
            segs.insert(0, (kl_ref[pl.ds(ws, WIN_KEYS), head], vl_ref[pl.ds(ws, WIN_KEYS), head], band))
        _store_heads(o_ref.at[:, kvh * g * HEAD_DIM:(kvh + 1) * g * HEAD_DIM], _softmax_pv_t(q, segs, extra=sink), g)

    if with_ctx:
        @pl.when(_is_ctx_step())
        def _():
            for kvh in range(KV_HEADS):
                run(kvh, False)

    @pl.when(jnp.logical_not(_is_ctx_step()))
    def _():
        for kvh in range(KV_HEADS):
            run(kvh, True)


def _neigh_kernel(q_ref, kl_ref, kc_ref, vl_ref, vc_ref, bias_ref, o_ref, *, with_ctx):
    i = pl.program_id(2)

    def run(h, lat):
        head = slice(h * HEAD_DIM, (h + 1) * HEAD_DIM)
        segs = [(kc_ref[:, head], vc_ref[:, head], None)]
        if lat:
            wr0 = jnp.clip(i * NA_QROWS - NA_WIN_H // 2, 0, ROWS - NA_KROWS)
            ws = pl.multiple_of(wr0 * GRID_W, GRID_W)
            add_bias = lambda s: s + bias_ref[h]
            segs.insert(0, (kl_ref[pl.ds(ws, NA_KEYS), head], vl_ref[pl.ds(ws, NA_KEYS), head], add_bias))
        o_ref[:, head] = _softmax_pv_t(q_ref[:, head], segs).T.astype(BF16)

    if with_ctx:
        @pl.when(_is_ctx_step())
        def _():
            for h in range(NA_HEADS_PER_STEP):
                run(h, False)

    @pl.when(jnp.logical_not(_is_ctx_step()))
    def _():
        for h in range(NA_HEADS_PER_STEP):
            run(h, True)


def _qknorm_kernel(q_ref, kl_ref, kc_ref, vl_ref, vc_ref, o_ref, *, with_ctx):
    g = GROUP_HEADS // KV_HEADS

    def run(kvh, lat):
        head = slice(kvh * HEAD_DIM, (kvh + 1) * HEAD_DIM)
        group = slice(kvh * g * HEAD_DIM, (kvh + 1) * g * HEAD_DIM)
        segs = [(kc_ref[:, head], vc_ref[:, head], None)]
        if lat:
            segs.insert(0, (kl_ref[:, head], vl_ref[:, head], None))
        _store_heads(o_ref.at[:, group], _softmax_pv_t(_stacked_heads(q_ref.at[:, group], g), segs), g)

    if with_ctx:
        @pl.when(_is_ctx_step())
        def _():
            for kvh in range(KV_HEADS):
                run(kvh, False)

    @pl.when(jnp.logical_not(_is_ctx_step()))
    def _():
        for kvh in range(KV_HEADS):
            run(kvh, True)


def _attention_call(kernel, name, p, *, q_col, k_col, v_col, q_heads, kv_heads, n_groups, with_ctx, pre=(), post=(),
                    vmem_mib=48):
    qw = q_heads * HEAD_DIM
    kvw = kv_heads * HEAD_DIM
    n_kv = n_groups
    ctx_blk0 = N_LAT // CTX_LEN
    steps = NQ + 1 if with_ctx else NQ
    q_spec = pl.BlockSpec((TQ, qw), lambda b, h, i: (_q_row_map(b, h, i), q_col // qw + h))
    kv_lat = lambda col: pl.BlockSpec((SEQ, kvw), lambda b, h, i: (b, col // kvw + h))
    kv_ctx = lambda col: pl.BlockSpec((CTX_LEN, kvw), lambda b, h, i: (ctx_blk0 + b, col // kvw + h))
    pre_specs, pre_args = zip(*pre) if pre else ((), ())
    post_specs, post_args = zip(*post) if post else ((), ())
    rows = N_TOK if with_ctx else N_LAT
    return pl.pallas_call(
        kernel,
        grid=(BATCH, n_kv, steps),
        in_specs=[*pre_specs, q_spec, kv_lat(k_col), kv_ctx(k_col), kv_lat(v_col), kv_ctx(v_col), *post_specs],
        out_specs=pl.BlockSpec((TQ, qw), lambda b, h, i: (_q_row_map(b, h, i), h)),
        out_shape=jax.ShapeDtypeStruct((rows, GROUP_WIDTH), BF16),
        compiler_params=_params(("arbitrary", "arbitrary", "arbitrary"), vmem_mib),
        name=name,
    )(*pre_args, p, p, p, p, p, *post_args)


def _const_spec(shape):
    return pl.BlockSpec(shape, lambda b, h, i: (0,) * len(shape))


def _neigh_bias(rpb):
    n_rel_r, n_rel_c = 2 * NA_WIN_H - 1, 2 * NA_WIN_W - 1
    ck = np.arange(GRID_W)[:, None]
    cq = np.arange(GRID_W)[None, :]
    cstart = np.clip(cq - NA_WIN_W // 2, 0, GRID_W - NA_WIN_W)
    col_ok = (ck >= cstart) & (ck < cstart + NA_WIN_W)
    rel_c = np.clip(ck - cq + NA_WIN_W - 1, 0, n_rel_c - 1)
    pick_c = (rel_c[None] == np.arange(n_rel_c)[:, None, None]).astype(np.float32)
    strip = jnp.einsum("dhrc,ckq->dhrkq", rpb.astype(F32), pick_c, precision=lax.Precision.HIGHEST) * LOG2E
    strip = jnp.where(col_ok, strip, NEG_INF).reshape(DEPTH, GROUP_HEADS, n_rel_r * GRID_W, GRID_W)

    def neg(n_rows):
        return jnp.full((DEPTH, GROUP_HEADS, n_rows * GRID_W, GRID_W), NEG_INF, F32)

    blocks = []
    for blk in (0, 1, NQ - 1):
        r0 = blk * NA_QROWS
        wr0 = int(np.clip(r0 - NA_WIN_H // 2, 0, ROWS - NA_KROWS))
        q_rows = []
        for rq in range(r0, r0 + NA_QROWS):
            rstart = int(np.clip(rq - NA_WIN_H // 2, 0, ROWS - NA_WIN_H))
            before = rstart - wr0
            after = NA_KROWS - NA_WIN_H - before
            rel0 = rstart - rq + NA_WIN_H - 1
            pieces = [neg(before)] if before else []
            pieces.append(strip[:, :, rel0 * GRID_W:(rel0 + NA_WIN_H) * GRID_W, :])
            if after:
                pieces.append(neg(after))
            q_rows.append(jnp.concatenate(pieces, axis=2))
        blocks.append(jnp.concatenate(q_rows, axis=3))
    return jnp.stack(blocks, axis=2)


def _outproj_kernel(*refs, with_ctx):
    mix_refs, gate_refs, refs = refs[:N_GROUPS], refs[N_GROUPS:2 * N_GROUPS], refs[2 * N_GROUPS:]
    if with_ctx:
        hl_ref, hc_ref, mod_ref, pg_ref, w_ref, outl_ref, outc_ref, a_ref = refs
    else:
        hl_ref, mod_ref, pg_ref, w_ref, outl_ref, a_ref = refs
    i = pl.program_id(0)
    lat_blocks = N_LAT // OUT_TM

    def block(h_ref, out_ref):
        row = jnp.minimum(i // (SEQ // OUT_TM), BATCH)
        gate = mod_ref[pl.ds(row, 1), 2 * D_MODEL:3 * D_MODEL]
        pg = pg_ref[...]
        slab = OUT_TM // OUT_SLABS
        for s in range(OUT_SLABS):
            sl = slice(s * slab, (s + 1) * slab)
            for n, (o, g) in enumerate(zip(mix_refs, gate_refs)):
                a_ref[sl, n * GROUP_WIDTH:(n + 1) * GROUP_WIDTH] = o[sl, :] * g[sl, :]
            y = jnp.dot(a_ref[sl, :], w_ref[...], preferred_element_type=F32)
            yn = y * lax.rsqrt(jnp.mean(y * y, axis=-1, keepdims=True) + EPS) * pg
            out_ref[sl, :] = h_ref[sl, :] + gate * yn

    if with_ctx:
        pl.when(i < lat_blocks)(functools.partial(block, hl_ref, outl_ref))
        pl.when(i >= lat_blocks)(functools.partial(block, hc_ref, outc_ref))
    else:
        block(hl_ref, outl_ref)


def _out_projection(l, outs, p, h_lat, h_ctx, mod, post_g, w_out, with_ctx):
    lat_blocks = N_LAT // OUT_TM
    rows = N_TOK if with_ctx else N_LAT
    o_spec = pl.BlockSpec((OUT_TM, GROUP_WIDTH), lambda i: (i, 0))
    gate_spec = lambda n: pl.BlockSpec((OUT_TM, GROUP_WIDTH), lambda i: (i, COL_GATE // GROUP_WIDTH + n))
    lat_spec = pl.BlockSpec((OUT_TM, D_MODEL), lambda i: (jnp.minimum(i, lat_blocks - 1), 0))
    ctx_spec = pl.BlockSpec((OUT_TM, D_MODEL), lambda i: (jnp.maximum(i - lat_blocks, 0), 0))
    residuals = [(lat_spec, h_lat)] + ([(ctx_spec, h_ctx)] if with_ctx else [])
    out_specs = [lat_spec] + ([ctx_spec] if with_ctx else [])
    out_shape = [jax.ShapeDtypeStruct((N_LAT, D_MODEL), F32)] + (
        [jax.ShapeDtypeStruct((N_CTX, D_MODEL), F32)] if with_ctx else [])
    res = pl.pallas_call(
        functools.partial(_outproj_kernel, with_ctx=with_ctx),
        grid=(rows // OUT_TM,),
        in_specs=[
            o_spec, o_spec, o_spec, o_spec,
            gate_spec(0), gate_spec(1), gate_spec(2), gate_spec(3),
            *[spec for spec, _ in residuals],
            pl.BlockSpec((None, 8, 3 * D_MODEL), lambda i: (l, 0, 0)),
            pl.BlockSpec((1, D_MODEL), lambda i: (0, 0)),
            pl.BlockSpec((None, MIX_WIDTH, D_MODEL), lambda i: (l, 0, 0), pipeline_mode=pl.Buffered(1)),
        ],
        out_specs=out_specs,
        out_shape=out_shape,
        scratch_shapes=[pltpu.VMEM((OUT_TM, MIX_WIDTH), BF16)],
        compiler_params=_params(("arbitrary",), 56),
        name="out_projection",
    )(*outs, p, p, p, p, *[a for _, a in residuals], mod, post_g.reshape(1, D_MODEL), w_out)
    return (res[0], res[1]) if with_ctx else (res[0], None)


def kernel(x, c, ctx, c_ctx, w_mod, b_mod, pre_norm_g, w_in, diff_lambda_q1, diff_lambda_k1, diff_lambda_q2,
           diff_lambda_k2, diff_subln_g, win_sink, na_rpb, qk_q_norm_g, qk_k_norm_g, w_out, post_norm_g):
    assert x.shape == (BATCH, SEQ, D_MODEL) and ctx.shape == (BATCH, CTX_LEN, D_MODEL)
    cc = jnp.concatenate([c, c_ctx[None], jnp.zeros((8 - BATCH - 1, D_MODEL), F32)], axis=0)
    mod = _modulation(cc, w_mod, b_mod)
    tables = _rope_tables()
    na_bias = _neigh_bias(na_rpb)
    w_in_b = w_in.astype(BF16)
    w_out_b = w_out.astype(BF16)
    h_lat = x.reshape(N_LAT, D_MODEL)
    h_ctx = ctx.reshape(N_CTX, D_MODEL)

    vec = lambda a, n: a.reshape(1, n)
    for l in range(DEPTH):
        with_ctx = l < DEPTH - 1
        lambda_init = 0.8 - 0.6 * math.exp(-0.3 * l)
        p = _in_projection(l, h_lat, h_ctx, mod, pre_norm_g[l], w_in_b, tables, qk_q_norm_g[l], qk_k_norm_g[l])

        lam_vecs = [(_const_spec((1, DIFF_DIM)), vec(a[l], DIFF_DIM))
                    for a in (diff_lambda_q1, diff_lambda_k1, diff_lambda_q2, diff_lambda_k2)]
        oa = _attention_call(
            functools.partial(_diff_kernel, lambda_init=lambda_init, with_ctx=with_ctx), "diff_attention", p,
            q_col=COL_A_Q, k_col=COL_A_K, v_col=COL_A_V, q_heads=DIFF_HEADS_PER_STEP,
            kv_heads=DIFF_HEADS_PER_STEP, n_groups=GROUP_HEADS // DIFF_HEADS_PER_STEP,
            with_ctx=with_ctx,
            post=lam_vecs + [(_const_spec((HEAD_DIM, 1)), diff_subln_g[l].reshape(HEAD_DIM, 1))])
        ob = _attention_call(
            functools.partial(_window_kernel, with_ctx=with_ctx), "window_attention", p,
            q_col=COL_B_Q, k_col=COL_B_K, v_col=COL_B_V, q_heads=GROUP_HEADS, kv_heads=KV_HEADS, n_groups=1,
            with_ctx=with_ctx, pre=[(pl.BlockSpec(memory_space=pltpu.SMEM), win_sink[l])])
        bias_spec = pl.BlockSpec(
            (None, NA_HEADS_PER_STEP, None, NA_KEYS, TQ),
            lambda b, h, i, l=l: (l, h, jnp.where(i == 0, 0, jnp.where(i >= NQ - 1, 2, 1)), 0, 0))
        on = _attention_call(
            functools.partial(_neigh_kernel, with_ctx=with_ctx), "neighborhood_attention", p,
            q_col=COL_C_Q, k_col=COL_C_K, v_col=COL_C_V, q_heads=NA_HEADS_PER_STEP, kv_heads=NA_HEADS_PER_STEP,
            n_groups=GROUP_HEADS // NA_HEADS_PER_STEP, with_ctx=with_ctx, post=[(bias_spec, na_bias)])
        od = _attention_call(
            functools.partial(_qknorm_kernel, with_ctx=with_ctx), "qknorm_attention", p,
            q_col=COL_D_Q, k_col=COL_D_K, v_col=COL_D_V, q_heads=GROUP_HEADS, kv_heads=KV_HEADS, n_groups=1,
            with_ctx=with_ctx)

        h_lat, h_ctx = _out_projection(l, (oa, ob, on, od), p, h_lat, h_ctx, mod, post_norm_g[l], w_out_b, with_ctx)
    return h_lat.reshape(BATCH, SEQ, D_MODEL)
```

```python
import functools
import math

import numpy as np
import jax
import jax.numpy as jnp
from jax import lax
from jax.experimental import pallas as pl
from jax.experimental.pallas import tpu as pltpu

D_MODEL = 2048
BATCH = 4
SEQ = 2048
DEPTH = 4
GRID_W = 64
CTX_LEN = 256
HEAD_DIM = 128
N_GROUPS = 4
GROUP_HEADS = 4
GROUP_WIDTH = GROUP_HEADS * HEAD_DIM
MIX_WIDTH = N_GROUPS * GROUP_WIDTH
KV_HEADS = 2
KV_WIDTH = KV_HEADS * HEAD_DIM
DIFF_DIM = HEAD_DIM // 2
WINDOW = 128
NA_WIN_H = 8
NA_WIN_W = 16
ROPE_THETA = 10000.0
EPS = 1e-6
NEG_INF = -1e30

IN_WIDTH = 3 * GROUP_WIDTH + (GROUP_WIDTH + 2 * KV_WIDTH) + 3 * GROUP_WIDTH + (GROUP_WIDTH + 2 * KV_WIDTH) + MIX_WIDTH
COL_A_Q, COL_A_K, COL_A_V = 0, 512, 1024
COL_B_Q, COL_B_K, COL_B_V = 1536, 2048, 2304
COL_C_Q, COL_C_K, COL_C_V = 2560, 3072, 3584
COL_D_Q, COL_D_K, COL_D_V = 4096, 4608, 4864
COL_GATE = 5120

N_LAT = BATCH * SEQ
N_CTX = BATCH * CTX_LEN
N_TOK = N_LAT + N_CTX
ROWS = SEQ // GRID_W

F32 = jnp.float32
BF16 = jnp.bfloat16

MIB = 1024 * 1024

TQ = 256
NQ = SEQ // TQ
IN_TM = 1024
IN_TN = 512
IN_SLABS = 4
OUT_TM = 512
OUT_SLABS = 2
MOD_TN = 768
NA_QROWS = TQ // GRID_W
NA_KROWS = 12
NA_KEYS = NA_KROWS * GRID_W
NA_HEADS_PER_STEP = 4
DIFF_HEADS_PER_STEP = 4
WIN_KEYS = TQ + 2 * WINDOW

_NT = (((1,), (1,)), ((), ()))
_TN = (((0,), (0,)), ((), ()))


def _params(semantics, vmem_mib):
    return pltpu.CompilerParams(dimension_semantics=semantics, vmem_limit_bytes=vmem_mib * MIB)


def _silu(x):
    return x / (1.0 + jnp.exp(-x))


def _mod_kernel(cc_ref, w_ref, b_ref, o_ref):
    sc = _silu(cc_ref[...])
    o_ref[...] = jnp.dot(sc.astype(BF16), w_ref[...].astype(BF16), preferred_element_type=F32) + b_ref[...]


def _modulation(cc, w_mod, b_mod):
    n = 3 * D_MODEL
    return pl.pallas_call(
        _mod_kernel,
        grid=(DEPTH, n // MOD_TN),
        in_specs=[
            pl.BlockSpec((8, D_MODEL), lambda l, j: (0, 0)),
            pl.BlockSpec((None, D_MODEL, MOD_TN), lambda l, j: (l, 0, j)),
            pl.BlockSpec((None, 1, MOD_TN), lambda l, j: (l, 0, j)),
        ],
        out_specs=pl.BlockSpec((None, 8, MOD_TN), lambda l, j: (l, 0, j)),
        out_shape=jax.ShapeDtypeStruct((DEPTH, 8, n), F32),
        compiler_params=_params(("arbitrary", "arbitrary"), 40),
        name="modulation",
    )(cc, w_mod, b_mod.reshape(DEPTH, 1, n))


_K_PLAIN, _K_ROPE_SUB_Q, _K_ROPE_SUB, _K_ROPE_Q, _K_ROPE, _K_SCALE, _K_NORM_ROPE_Q, _K_NORM_ROPE, _K_SILU = range(9)
_SECTIONS = (
    (COL_A_Q, COL_A_K, _K_ROPE_SUB_Q), (COL_A_K, COL_A_V, _K_ROPE_SUB), (COL_A_V, COL_B_Q, _K_PLAIN),
    (COL_B_Q, COL_B_K, _K_ROPE_Q), (COL_B_K, COL_B_V, _K_ROPE), (COL_B_V, COL_C_Q, _K_PLAIN),
    (COL_C_Q, COL_C_K, _K_SCALE), (COL_C_K, COL_D_Q, _K_PLAIN),
    (COL_D_Q, COL_D_K, _K_NORM_ROPE_Q), (COL_D_K, COL_D_V, _K_NORM_ROPE), (COL_D_V, COL_GATE, _K_PLAIN),
    (COL_GATE, IN_WIDTH, _K_SILU),
)
LOG2E = math.log2(math.e)
_HEAD_SCALE = HEAD_DIM ** -0.5 * LOG2E
_SUB_SCALE = DIFF_DIM ** -0.5 * LOG2E


_UNIT_KINDS = [kind for lo, hi, kind in _SECTIONS for _ in range((hi - lo) // HEAD_DIM)]
_TILE_UNITS = IN_TN // HEAD_DIM
_TILE_PATTERNS = [tuple(_UNIT_KINDS[t * _TILE_UNITS:(t + 1) * _TILE_UNITS]) for t in range(IN_WIDTH // IN_TN)]


def _tile_pred(j, pattern):
    pred = None
    for t, pat in enumerate(_TILE_PATTERNS):
        if pat == pattern:
            pred = (j == t) if pred is None else jnp.logical_or(pred, j == t)
    return pred


def _head_rms(x, g):
    return x * lax.rsqrt(jnp.mean(x * x, axis=-1, keepdims=True) + EPS) * g


def _inproj_kernel(xl_ref, xc_ref, mod_ref, g_ref, w_ref, ch_ref, sh_ref, cs_ref, sa_ref, sb_ref, gq_ref, gk_ref,
                   o_ref, n_ref):
    i = pl.program_id(0)
    j = pl.program_id(1)
    lat_blocks = N_LAT // IN_TM

    def normed(x_ref, sl):
        row = jnp.minimum(i // (SEQ // IN_TM), BATCH)
        shift = mod_ref[pl.ds(row, 1), 0:D_MODEL]
        scale1 = 1.0 + mod_ref[pl.ds(row, 1), D_MODEL:2 * D_MODEL]
        x = x_ref[sl, :]
        inv = lax.rsqrt(jnp.mean(x * x, axis=-1, keepdims=True) + EPS)
        n = (x * inv * g_ref[...] * scale1 + shift).astype(BF16)
        n_ref[sl, :] = n
        return n

    def rope_head(x, sl):
        return x * ch_ref[sl, :] + pltpu.roll(x, HEAD_DIM // 2, 1) * sh_ref[sl, :]

    def rope_sub(x, sl):
        return (x * cs_ref[sl, :] + pltpu.roll(x, HEAD_DIM - DIFF_DIM // 2, 1) * sa_ref[sl, :]
                + pltpu.roll(x, DIFF_DIM // 2, 1) * sb_ref[sl, :])

    unit_fn = {
        _K_PLAIN: lambda x, sl: x,
        _K_ROPE_SUB_Q: lambda x, sl: rope_sub(x, sl) * _SUB_SCALE,
        _K_ROPE_SUB: rope_sub,
        _K_ROPE_Q: lambda x, sl: rope_head(x, sl) * _HEAD_SCALE,
        _K_ROPE: rope_head,
        _K_SCALE: lambda x, sl: x * _HEAD_SCALE,
        _K_NORM_ROPE_Q: lambda x, sl: rope_head(_head_rms(x, gq_ref[...]), sl) * _HEAD_SCALE,
        _K_NORM_ROPE: lambda x, sl: rope_head(_head_rms(x, gk_ref[...]), sl),
        _K_SILU: lambda x, sl: _silu(x),
    }

    assert _TILE_PATTERNS.count(_TILE_PATTERNS[0]) == 1, "the pre-norm rides on a pattern only tile 0 has"
    slab = IN_TM // IN_SLABS
    def tile(pattern, x_ref):
        for s in range(IN_SLABS):
            sl = slice(s * slab, (s + 1) * slab)
            lhs = n_ref[sl, :] if x_ref is None else normed(x_ref, sl)
            acc = jnp.dot(lhs, w_ref[...], preferred_element_type=F32)
            for u, kind in enumerate(pattern):
                cols = slice(u * HEAD_DIM, (u + 1) * HEAD_DIM)
                o_ref[sl, cols] = unit_fn[kind](acc[:, cols], sl).astype(BF16)

    for pattern in dict.fromkeys(_TILE_PATTERNS):
        pred = _tile_pred(j, pattern)
        if pattern == _TILE_PATTERNS[0]:
            pl.when(jnp.logical_and(pred, i < lat_blocks))(functools.partial(tile, pattern, xl_ref))
            pl.when(jnp.logical_and(pred, i >= lat_blocks))(functools.partial(tile, pattern, xc_ref))
        else:
            pl.when(pred)(functools.partial(tile, pattern, None))


def _in_projection(l, h_lat, h_ctx, mod, pre_g, w_in, tables, gq, gk):
    assert N_CTX == IN_TM, "the context tokens are exactly the last row block"
    lat_blocks = N_LAT // IN_TM
    per_seq = SEQ // IN_TM

    def tab_map(i, j):
        return (jnp.where(i < lat_blocks, i % per_seq, per_seq), 0)

    tab_spec = pl.BlockSpec((IN_TM, HEAD_DIM), tab_map)
    vec_spec = lambda n: pl.BlockSpec((1, n), lambda i, j: (0, 0))
    return pl.pallas_call(
        _inproj_kernel,
        grid=(N_TOK // IN_TM, IN_WIDTH // IN_TN),
        in_specs=[
            pl.BlockSpec((IN_TM, D_MODEL), lambda i, j: (jnp.minimum(i, lat_blocks - 1), 0)),
            pl.BlockSpec((IN_TM, D_MODEL), lambda i, j: (0, 0), pipeline_mode=pl.Buffered(1)),
            pl.BlockSpec((None, 8, 3 * D_MODEL), lambda i, j: (l, 0, 0)),
            vec_spec(D_MODEL),
            pl.BlockSpec((None, D_MODEL, IN_TN), lambda i, j: (l, 0, j)),
            tab_spec, tab_spec, tab_spec, tab_spec, tab_spec,
            vec_spec(HEAD_DIM), vec_spec(HEAD_DIM),
        ],
        out_specs=pl.BlockSpec((IN_TM, IN_TN), lambda i, j: (i, j)),
        out_shape=jax.ShapeDtypeStruct((N_TOK, IN_WIDTH), BF16),
        scratch_shapes=[pltpu.VMEM((IN_TM, D_MODEL), BF16)],
        compiler_params=_params(("arbitrary", "arbitrary"), 56),
        name="in_projection",
    )(h_lat, h_ctx, mod, pre_g.reshape(1, D_MODEL), w_in, *tables, gq.reshape(1, HEAD_DIM), gk.reshape(1, HEAD_DIM))


def _rope_tables():
    t = jnp.arange(SEQ)
    row = (t // GRID_W).astype(F32)
    col = (t % GRID_W).astype(F32)

    def cos_sin(dim):
        n_freq = dim // 4
        inv = ROPE_THETA ** (-jnp.arange(n_freq, dtype=F32) / n_freq)
        ang = jnp.concatenate([row[:, None] * inv, col[:, None] * inv], axis=-1)
        return jnp.cos(ang), jnp.sin(ang)

    ch, sh = cos_sin(HEAD_DIM)
    cs, ss = cos_sin(DIFF_DIM)
    zs = jnp.zeros_like(ss)
    tabs = [
        jnp.concatenate([ch, ch], axis=-1),
        jnp.concatenate([-sh, sh], axis=-1),
        jnp.concatenate([cs, cs, cs, cs], axis=-1),
        jnp.concatenate([-ss, zs, -ss, zs], axis=-1),
        jnp.concatenate([zs, ss, zs, ss], axis=-1),
    ]
    ident = [jnp.ones, jnp.zeros, jnp.ones, jnp.zeros, jnp.zeros]
    return [jnp.concatenate([tb, f((IN_TM, HEAD_DIM), F32)], axis=0) for tb, f in zip(tabs, ident)]


def _scores_t(q, segs):
    scores = []
    for k, v, fix in segs:
        s = lax.dot_general(k, q, _NT, preferred_element_type=F32)
        scores.append(s if fix is None else fix(s))
    return scores


def _softmax_pv_t(scores, segs, extra=None):
    m = extra
    for s in scores:
        sm = jnp.max(s, axis=0, keepdims=True)
        m = sm if m is None else jnp.maximum(m, sm)
    l = None if extra is None else jnp.exp2(extra - m)
    o = None
    for s, (k, v, fix) in zip(scores, segs):
        e = jnp.exp2(s - m)
        ls = jnp.sum(e, axis=0, keepdims=True)
        l = ls if l is None else l + ls
        pv = lax.dot_general(v, e.astype(BF16), _TN, preferred_element_type=F32)
        o = pv if o is None else o + pv
    return o / l


def _run_chains(chains):
    pending = None
    for q, segs, extra, store in chains:
        scores = _scores_t(q, segs)
        if pending is not None:
            pending()
        pending = functools.partial(
            lambda scores, segs, extra, store: store(_softmax_pv_t(scores, segs, extra)), scores, segs, extra, store)
    pending()


def _stacked_heads(q_ref, g):
    return jnp.concatenate([q_ref[:, h * HEAD_DIM:(h + 1) * HEAD_DIM] for h in range(g)], axis=0)


def _store_heads(o_ref, ot, g):
    for h in range(g):
        o_ref[:, h * HEAD_DIM:(h + 1) * HEAD_DIM] = ot[:, h * TQ:(h + 1) * TQ].T.astype(BF16)


def _is_ctx_step():
    return pl.program_id(2) == NQ


def _q_row_map(b, h, i):
    return jnp.where(i < NQ, b * NQ + i, BATCH * NQ + b)


def _steps(with_ctx, n_chains, chain):
    if with_ctx:
        @pl.when(_is_ctx_step())
        def _():
            _run_chains([chain(n, False) for n in range(n_chains)])

    @pl.when(jnp.logical_not(_is_ctx_step()))
    def _():
        _run_chains([chain(n, True) for n in range(n_chains)])


def _diff_kernel(q_ref, kl_ref, kc_ref, vl_ref, vc_ref, lq1_ref, lk1_ref, lq2_ref, lk2_ref, g_ref, o_ref, *,
                 lambda_init, with_ctx):
    lam = (jnp.exp(jnp.sum(lq1_ref[...] * lk1_ref[...], axis=-1, keepdims=True))
           - jnp.exp(jnp.sum(lq2_ref[...] * lk2_ref[...], axis=-1, keepdims=True)) + lambda_init)

    def chain(h, lat):
        head = slice(h * HEAD_DIM, (h + 1) * HEAD_DIM)
        q = q_ref[:, head]
        lane = lax.broadcasted_iota(jnp.int32, q.shape, 1)
        zero = jnp.zeros_like(q)
        qq = jnp.concatenate([jnp.where(lane < DIFF_DIM, q, zero), jnp.where(lane >= DIFF_DIM, q, zero)], axis=0)
        segs = [(kc_ref[:, head], vc_ref[:, head], None)]
        if lat:
            segs.insert(0, (kl_ref[:, head], vl_ref[:, head], None))

        def store(ot):
            ot = ot[:, :TQ] - lam * ot[:, TQ:]
            ot = ot * lax.rsqrt(jnp.mean(ot * ot, axis=0, keepdims=True) + EPS) * g_ref[...]
            o_ref[:, head] = (ot * (1.0 - lambda_init)).T.astype(BF16)

        return qq, segs, None, store

    _steps(with_ctx, DIFF_HEADS_PER_STEP, chain)


def _window_kernel(sink_ref, q_ref, kl_ref, kc_ref, vl_ref, vc_ref, o_ref, *, with_ctx):
    i = pl.program_id(2)
    g = GROUP_HEADS // KV_HEADS
    cols = lax.broadcasted_iota(jnp.int32, (1, g * TQ), 1)

    def chain(kvh, lat):
        head = slice(kvh * HEAD_DIM, (kvh + 1) * HEAD_DIM)
        group = slice(kvh * g * HEAD_DIM, (kvh + 1) * g * HEAD_DIM)
        sink = jnp.where(cols < TQ, sink_ref[kvh * g], sink_ref[kvh * g + 1]).astype(F32) * LOG2E
        segs = [(kc_ref[:, head], vc_ref[:, head], None)]
        if lat:
            q0 = i * TQ
            ws = pl.multiple_of(jnp.clip(q0 - WINDOW, 0, SEQ - WIN_KEYS), WINDOW)
            kpos = ws + lax.broadcasted_iota(jnp.int32, (WIN_KEYS, g * TQ), 0)
            qpos = q0 + (lax.broadcasted_iota(jnp.int32, (WIN_KEYS, g * TQ), 1) & (TQ - 1))
            valid = jnp.abs(kpos - qpos) <= WINDOW
            band = lambda s: jnp.where(valid, s, NEG_INF)
            segs.insert(0, (kl_ref[pl.ds(ws, WIN_KEYS), head], vl_ref[pl.ds(ws, WIN_KEYS), head], band))
        store = lambda ot: _store_heads(o_ref.at[:, group], ot, g)
        return _stacked_heads(q_ref.at[:, group], g), segs, sink, store

    _steps(with_ctx, KV_HEADS, chain)


def _neigh_kernel(q_ref, kl_ref, kc_ref, vl_ref, vc_ref, bias_ref, o_ref, *, with_ctx):
    i = pl.program_id(2)

    def chain(h, lat):
        head = slice(h * HEAD_DIM, (h + 1) * HEAD_DIM)
        segs = [(kc_ref[:, head], vc_ref[:, head], None)]
        if lat:
            wr0 = jnp.clip(i * NA_QROWS - NA_WIN_H // 2, 0, ROWS - NA_KROWS)
            ws = pl.multiple_of(wr0 * GRID_W, GRID_W)
            add_bias = lambda s: s + bias_ref[h]
            segs.insert(0, (kl_ref[pl.ds(ws, NA_KEYS), head], vl_ref[pl.ds(ws, NA_KEYS), head], add_bias))

        def store(ot):
            o_ref[:, head] = ot.T.astype(BF16)

        return q_ref[:, head], segs, None, store

    _steps(with_ctx, NA_HEADS_PER_STEP, chain)


def _qknorm_kernel(q_ref, kl_ref, kc_ref, vl_ref, vc_ref, o_ref, *, with_ctx):
    g = GROUP_HEADS // KV_HEADS

    def chain(kvh, lat):
        head = slice(kvh * HEAD_DIM, (kvh + 1) * HEAD_DIM)
        group = slice(kvh * g * HEAD_DIM, (kvh + 1) * g * HEAD_DIM)
        segs = [(kc_ref[:, head], vc_ref[:, head], None)]
        if lat:
            segs.insert(0, (kl_ref[:, head], vl_ref[:, head], None))
        store = lambda ot: _store_heads(o_ref.at[:, group], ot, g)
        return _stacked_heads(q_ref.at[:, group], g), segs, None, store

    _steps(with_ctx, KV_HEADS, chain)


def _attention_call(kernel, name, p, *, q_col, k_col, v_col, q_heads, kv_heads, n_groups, with_ctx, pre=(), post=(),
                    vmem_mib=48):
    qw = q_heads * HEAD_DIM
    kvw = kv_heads * HEAD_DIM
    n_kv = n_groups
    ctx_blk0 = N_LAT // CTX_LEN
    steps = NQ + 1 if with_ctx else NQ
    q_spec = pl.BlockSpec((TQ, qw), lambda b, h, i: (_q_row_map(b, h, i), q_col // qw + h))
    kv_lat = lambda col: pl.BlockSpec((SEQ, kvw), lambda b, h, i: (b, col // kvw + h))
    kv_ctx = lambda col: pl.BlockSpec((CTX_LEN, kvw), lambda b, h, i: (ctx_blk0 + b, col // kvw + h))
    pre_specs, pre_args = zip(*pre) if pre else ((), ())
    post_specs, post_args = zip(*post) if post else ((), ())
    rows = N_TOK if with_ctx else N_LAT
    return pl.pallas_call(
        kernel,
        grid=(BATCH, n_kv, steps),
        in_specs=[*pre_specs, q_spec, kv_lat(k_col), kv_ctx(k_col), kv_lat(v_col), kv_ctx(v_col), *post_specs],
        out_specs=pl.BlockSpec((TQ, qw), lambda b, h, i: (_q_row_map(b, h, i), h)),
        out_shape=jax.ShapeDtypeStruct((rows, GROUP_WIDTH), BF16),
        compiler_params=_params(("arbitrary", "arbitrary", "arbitrary"), vmem_mib),
        name=name,
    )(*pre_args, p, p, p, p, p, *post_args)


def _const_spec(shape):
    return pl.BlockSpec(shape, lambda b, h, i: (0,) * len(shape))


def _neigh_bias(rpb):
    n_rel_r, n_rel_c = 2 * NA_WIN_H - 1, 2 * NA_WIN_W - 1
    ck = np.arange(GRID_W)[:, None]
    cq = np.arange(GRID_W)[None, :]
    cstart = np.clip(cq - NA_WIN_W // 2, 0, GRID_W - NA_WIN_W)
    col_ok = (ck >= cstart) & (ck < cstart + NA_WIN_W)
    rel_c = np.clip(ck - cq + NA_WIN_W - 1, 0, n_rel_c - 1)
    pick_c = (rel_c[None] == np.arange(n_rel_c)[:, None, None]).astype(np.float32)
    strip = jnp.einsum("dhrc,ckq->dhrkq", rpb.astype(F32), pick_c, precision=lax.Precision.HIGHEST) * LOG2E
    strip = jnp.where(col_ok, strip, NEG_INF).reshape(DEPTH, GROUP_HEADS, n_rel_r * GRID_W, GRID_W)

    def neg(n_rows):
        return jnp.full((DEPTH, GROUP_HEADS, n_rows * GRID_W, GRID_W), NEG_INF, F32)

    blocks = []
    for blk in (0, 1, NQ - 1):
        r0 = blk * NA_QROWS
        wr0 = int(np.clip(r0 - NA_WIN_H // 2, 0, ROWS - NA_KROWS))
        q_rows = []
        for rq in range(r0, r0 + NA_QROWS):
            rstart = int(np.clip(rq - NA_WIN_H // 2, 0, ROWS - NA_WIN_H))
            before = rstart - wr0
            after = NA_KROWS - NA_WIN_H - before
            rel0 = rstart - rq + NA_WIN_H - 1
            pieces = [neg(before)] if before else []
            pieces.append(strip[:, :, rel0 * GRID_W:(rel0 + NA_WIN_H) * GRID_W, :])
            if after:
                pieces.append(neg(after))
            q_rows.append(jnp.concatenate(pieces, axis=2))
        blocks.append(jnp.concatenate(q_rows, axis=3))
    return jnp.stack(blocks, axis=2)


def _outproj_kernel(*refs, with_ctx):
    mix_refs, gate_refs, refs = refs[:N_GROUPS], refs[N_GROUPS:2 * N_GROUPS], refs[2 * N_GROUPS:]
    if with_ctx:
        hl_ref, hc_ref, mod_ref, pg_ref, w_ref, outl_ref, outc_ref, a_ref = refs
    else:
        hl_ref, mod_ref, pg_ref, w_ref, outl_ref, a_ref = refs
    i = pl.program_id(0)
    lat_blocks = N_LAT // OUT_TM

    def block(h_ref, out_ref):
        row = jnp.minimum(i // (SEQ // OUT_TM), BATCH)
        gate = mod_ref[pl.ds(row, 1), 2 * D_MODEL:3 * D_MODEL]
        pg = pg_ref[...]
        slab = OUT_TM // OUT_SLABS
        for s in range(OUT_SLABS):
            sl = slice(s * slab, (s + 1) * slab)
            for n, (o, g) in enumerate(zip(mix_refs, gate_refs)):
                a_ref[sl, n * GROUP_WIDTH:(n + 1) * GROUP_WIDTH] = o[sl, :] * g[sl, :]
            y = jnp.dot(a_ref[sl, :], w_ref[...], preferred_element_type=F32)
            yn = y * lax.rsqrt(jnp.mean(y * y, axis=-1, keepdims=True) + EPS) * pg
            out_ref[sl, :] = h_ref[sl, :] + gate * yn

    if with_ctx:
        pl.when(i < lat_blocks)(functools.partial(block, hl_ref, outl_ref))
        pl.when(i >= lat_blocks)(functools.partial(block, hc_ref, outc_ref))
    else:
        block(hl_ref, outl_ref)


def _out_projection(l, outs, p, h_lat, h_ctx, mod, post_g, w_out, with_ctx):
    lat_blocks = N_LAT // OUT_TM
    rows = N_TOK if with_ctx else N_LAT
    o_spec = pl.BlockSpec((OUT_TM, GROUP_WIDTH), lambda i: (i, 0))
    gate_spec = lambda n: pl.BlockSpec((OUT_TM, GROUP_WIDTH), lambda i: (i, COL_GATE // GROUP_WIDTH + n))
    lat_spec = pl.BlockSpec((OUT_TM, D_MODEL), lambda i: (jnp.minimum(i, lat_blocks - 1), 0))
    ctx_spec = pl.BlockSpec((OUT_TM, D_MODEL), lambda i: (jnp.maximum(i - lat_blocks, 0), 0))
    residuals = [(lat_spec, h_lat)] + ([(ctx_spec, h_ctx)] if with_ctx else [])
    out_specs = [lat_spec] + ([ctx_spec] if with_ctx else [])
    out_shape = [jax.ShapeDtypeStruct((N_LAT, D_MODEL), F32)] + (
        [jax.ShapeDtypeStruct((N_CTX, D_MODEL), F32)] if with_ctx else [])
    res = pl.pallas_call(
        functools.partial(_outproj_kernel, with_ctx=with_ctx),
        grid=(rows // OUT_TM,),
        in_specs=[
            o_spec, o_spec, o_spec, o_spec,
            gate_spec(0), gate_spec(1), gate_spec(2), gate_spec(3),
            *[spec for spec, _ in residuals],
            pl.BlockSpec((None, 8, 3 * D_MODEL), lambda i: (l, 0, 0)),
            pl.BlockSpec((1, D_MODEL), lambda i: (0, 0)),
            pl.BlockSpec((None, MIX_WIDTH, D_MODEL), lambda i: (l, 0, 0), pipeline_mode=pl.Buffered(1)),
        ],
        out_specs=out_specs,
        out_shape=out_shape,
        scratch_shapes=[pltpu.VMEM((OUT_TM, MIX_WIDTH), BF16)],
        compiler_params=_params(("arbitrary",), 56),
        name="out_projection",
    )(*outs, p, p, p, p, *[a for _, a in residuals], mod, post_g.reshape(1, D_MODEL), w_out)
    return (res[0], res[1]) if with_ctx else (res[0], None)


def kernel(x, c, ctx, c_ctx, w_mod, b_mod, pre_norm_g, w_in, diff_lambda_q1, diff_lambda_k1, diff_lambda_q2,
           diff_lambda_k2, diff_subln_g, win_sink, na_rpb, qk_q_norm_g, qk_k_norm_g, w_out, post_norm_g):
    assert x.shape == (BATCH, SEQ, D_MODEL) and ctx.shape == (BATCH, CTX_LEN, D_MODEL)
    cc = jnp.concatenate([c, c_ctx[None], jnp.zeros((8 - BATCH - 1, D_MODEL), F32)], axis=0)
    mod = _modulation(cc, w_mod, b_mod)
    tables = _rope_tables()
    na_bias = _neigh_bias(na_rpb)
    w_in_b = w_in.astype(BF16)
    w_out_b = w_out.astype(BF16)
    h_lat = x.reshape(N_LAT, D_MODEL)
    h_ctx = ctx.reshape(N_CTX, D_MODEL)

    vec = lambda a, n: a.reshape(1, n)
    for l in range(DEPTH):
        with_ctx = l < DEPTH - 1
        lambda_init = 0.8 - 0.6 * math.exp(-0.3 * l)
        p = _in_projection(l, h_lat, h_ctx, mod, pre_norm_g[l], w_in_b, tables, qk_q_norm_g[l], qk_k_norm_g[l])

        lam_vecs = [(_const_spec((1, DIFF_DIM)), vec(a[l], DIFF_DIM))
                    for a in (diff_lambda_q1, diff_lambda_k1, diff_lambda_q2, diff_lambda_k2)]
        oa = _attention_call(
            functools.partial(_diff_kernel, lambda_init=lambda_init, with_ctx=with_ctx), "diff_attention", p,
            q_col=COL_A_Q, k_col=COL_A_K, v_col=COL_A_V, q_heads=DIFF_HEADS_PER_STEP,
            kv_heads=DIFF_HEADS_PER_STEP, n_groups=GROUP_HEADS // DIFF_HEADS_PER_STEP,
            with_ctx=with_ctx,
            post=lam_vecs + [(_const_spec((HEAD_DIM, 1)), diff_subln_g[l].reshape(HEAD_DIM, 1))])
        ob = _attention_call(
            functools.partial(_window_kernel, with_ctx=with_ctx), "window_attention", p,
            q_col=COL_B_Q, k_col=COL_B_K, v_col=COL_B_V, q_heads=GROUP_HEADS, kv_heads=KV_HEADS, n_groups=1,
            with_ctx=with_ctx, pre=[(pl.BlockSpec(memory_space=pltpu.SMEM), win_sink[l])])
        bias_spec = pl.BlockSpec(
            (None, NA_HEADS_PER_STEP, None, NA_KEYS, TQ),
            lambda b, h, i, l=l: (l, h, jnp.where(i == 0, 0, jnp.where(i >= NQ - 1, 2, 1)), 0, 0))
        on = _attention_call(
            functools.partial(_neigh_kernel, with_ctx=with_ctx), "neighborhood_attention", p,
            q_col=COL_C_Q, k_col=COL_C_K, v_col=COL_C_V, q_heads=NA_HEADS_PER_STEP, kv_heads=NA_HEADS_PER_STEP,
            n_groups=GROUP_HEADS // NA_HEADS_PER_STEP, with_ctx=with_ctx, post=[(bias_spec, na_bias)])
        od = _attention_call(
            functools.partial(_qknorm_kernel, with_ctx=with_ctx), "qknorm_attention", p,
            q_col=COL_D_Q, k_col=COL_D_K, v_col=COL_D_V, q_heads=GROUP_HEADS, kv_heads=KV_HEADS, n_groups=1,
            with_ctx=with_ctx)

        h_lat, h_ctx = _out_projection(l, (oa, ob, on, od), p, h_lat, h_ctx, mod, post_norm_g[l], w_out_b, with_ctx)
    return h_lat.reshape(BATCH, SEQ, D_MODEL)
```
